```python
import math
import jax
import jax.numpy as jnp
from jax import lax
import numpy as np

D_MODEL = 4096
BATCH = 1
SEQ = 8192
DEPTH = 1
DEC_BATCH = 128
DEC_SEQ = 4
PAST_LEN = 2048
PAGE_SIZE = 128

N_META = 16
SB_HEAD_DIM = 128
SB_HEADS = D_MODEL // 256
SB_WIDTH = SB_HEADS * SB_HEAD_DIM
SB_SCALE = 1.0 / math.sqrt(SB_HEAD_DIM)
SB_BIAS_INIT = -6.0
Q_BLOCK = 128
POOL_WINDOWS = (2, 4, 8, 16)
N_POOL_GROUPS = len(POOL_WINDOWS)
POOL_GROUP_DIM = D_MODEL // 8
POOL_WIDTH = N_POOL_GROUPS * POOL_GROUP_DIM
POOL_CTX = max(POOL_WINDOWS) - 1
IN_WIDTH = 3 * SB_WIDTH + POOL_WIDTH + 2 * D_MODEL
D_FF = 4 * D_MODEL
NORM_EPS = 1e-6

kernel_name = 'stickbreak_pool_gated_hybrid_step'


def rmsnorm(x, g):
    xf = x.astype(jnp.float32)
    y = xf * lax.rsqrt(jnp.mean(xf * xf, axis=-1, keepdims=True) + NORM_EPS)
    return (y * g.astype(jnp.float32)).astype(x.dtype)


def in_proj(h, w_in, gate_bias):
    b, t, _ = h.shape
    p = jnp.einsum('btd,de->bte', h, w_in)
    q, k, v, u, g = jnp.split(p, [SB_WIDTH, 2 * SB_WIDTH, 3 * SB_WIDTH, 3 * SB_WIDTH + POOL_WIDTH], axis=-1)
    heads = (b, t, SB_HEADS, SB_HEAD_DIM)
    gates = jax.nn.sigmoid((g + gate_bias).astype(jnp.float32)).astype(h.dtype)
    return (q.reshape(heads), k.reshape(heads), v.reshape(heads), u,
            gates[..., :D_MODEL], gates[..., D_MODEL:])


def sb_weights(scores, mask, sb_bias):
    z = scores.astype(jnp.float32) * SB_SCALE + sb_bias.astype(jnp.float32)[None, :, None, None]
    log_fail = jnp.where(mask, jax.nn.log_sigmoid(-z), 0.0)
    log_later = lax.cumsum(log_fail, axis=z.ndim - 1, reverse=True) - log_fail
    return jnp.where(mask, jnp.exp(jax.nn.log_sigmoid(z) + log_later), 0.0)


def sb_attend(q, k, v, q_pos, k_pos, sb_bias):
    a = sb_weights(jnp.einsum('bqhd,bkhd->bhqk', q, k), k_pos[None, :] < q_pos[:, None], sb_bias)
    return jnp.einsum('bhqk,bkhd->bqhd', a, v.astype(jnp.float32)).astype(q.dtype)


def prompt_attention(q, k, v, sb_bias):
    b, t, nh, dh = q.shape
    pos = jnp.arange(t)
    o_meta = sb_attend(q[:, :N_META], k[:, :N_META], v[:, :N_META], pos[:N_META], pos[:N_META], sb_bias)
    n_blk = (t - N_META) // Q_BLOCK
    q_blk = q[:, N_META:].reshape(b, n_blk, Q_BLOCK, nh, dh).transpose(1, 0, 2, 3, 4)

    def one_block(args):
        qb, blk = args
        q_pos = N_META + blk * Q_BLOCK + jnp.arange(Q_BLOCK)
        return sb_attend(qb, k, v, q_pos, pos, sb_bias)

    o_real = lax.map(one_block, (q_blk, jnp.arange(n_blk)))
    o_real = o_real.transpose(1, 0, 2, 3, 4).reshape(b, t - N_META, nh, dh)
    return jnp.concatenate([o_meta, o_real], axis=1)


def sample_attention(q, k, v, k_pool, v_pool, page_table, sb_bias):
    db, s, nh, dh = q.shape
    k_past = k_pool[page_table].reshape(db, -1, nh, dh)
    v_past = v_pool[page_table].reshape(db, -1, nh, dh)
    p = k_past.shape[1]
    q_pos = p + jnp.arange(s)
    k_pos = jnp.concatenate([jnp.arange(p), q_pos])
    scores = jnp.concatenate([jnp.einsum('bqhd,bkhd->bhqk', q, k_past),
                              jnp.einsum('bqhd,bkhd->bhqk', q, k)], axis=-1)
    a = sb_weights(scores, k_pos[None, :] < q_pos[:, None], sb_bias)
    o = (jnp.einsum('bhqk,bkhd->bqhd', a[..., :p], v_past.astype(jnp.float32))
         + jnp.einsum('bhqk,bkhd->bqhd', a[..., p:], v.astype(jnp.float32)))
    return o.astype(q.dtype)


def pool_mix(u_ext, w_pool, pool_scale):
    b, l, _ = u_ext.shape
    uf = u_ext.astype(jnp.float32)
    cs = jnp.concatenate([jnp.zeros_like(uf[:, :1]), lax.cumsum(uf, axis=1)], axis=1)
    r = jnp.arange(l)
    means = []
    for gi, w in enumerate(POOL_WINDOWS):
        sl = slice(gi * POOL_GROUP_DIM, (gi + 1) * POOL_GROUP_DIM)
        lo = jnp.maximum(r + 1 - w, 0)
        win_sum = cs[:, 1:, sl] - cs[:, lo, sl]
        cnt = jnp.minimum(r + 1, w).astype(jnp.float32)[None, :, None]
        means.append(win_sum / cnt)
    pooled = (jnp.concatenate(means, axis=-1) - uf).astype(u_ext.dtype)
    pooled = pooled.reshape(b, l, N_POOL_GROUPS, POOL_GROUP_DIM)
    y = jnp.einsum('blgc,gcd->blgd', pooled, w_pool).reshape(b, l, POOL_WIDTH)
    return y * pool_scale


def merge_and_mlp(x, o_attn, o_pool, g_attn, g_pool, w_up_attn, w_up_pool, w_out,
                  g_post_mix, g_pre_mlp, w_ff1, w_ff2, g_post_mlp):
    b, t, _ = x.shape
    y_attn = jnp.einsum('btc,cd->btd', o_attn.reshape(b, t, SB_WIDTH), w_up_attn)
    y_pool = jnp.einsum('btc,cd->btd', o_pool, w_up_pool)
    mix = jnp.einsum('btc,cd->btd', g_attn * y_attn + g_pool * y_pool, w_out)
    x = x + rmsnorm(mix, g_post_mix)
    hid = jnp.square(jax.nn.relu(jnp.einsum('btd,df->btf', rmsnorm(x, g_pre_mlp), w_ff1)))
    return x + rmsnorm(jnp.einsum('btf,fd->btd', hid, w_ff2), g_post_mlp)


def setup_inputs(seed: int = 0) -> dict:
    key = jax.random.key(seed)
    ks = jax.random.split(key, 24)
    f32 = jnp.float32
    n_pages = PAST_LEN // PAGE_SIZE
    used = DEC_BATCH * n_pages
    n_phys = used + max(1, used // 4)
    nrm = lambda k, shape, s: jax.random.normal(k, shape, f32) * s
    gain = lambda k: 1.0 + nrm(k, (DEPTH, D_MODEL), 0.02)
    page_table = jax.random.permutation(ks[0], n_phys)[:used].reshape(DEC_BATCH, n_pages).astype(jnp.int32)
    return {
        'x_prompt': nrm(ks[1], (BATCH, SEQ, D_MODEL), 1.0),
        'x_sample': nrm(ks[2], (DEC_BATCH, DEC_SEQ, D_MODEL), 1.0),
        'cache_k': nrm(ks[3], (DEPTH, n_phys, PAGE_SIZE, SB_HEADS, SB_HEAD_DIM), 1.0),
        'cache_v': nrm(ks[4], (DEPTH, n_phys, PAGE_SIZE, SB_HEADS, SB_HEAD_DIM), 1.0),
        'state_pool': nrm(ks[5], (DEPTH, DEC_BATCH, POOL_CTX, POOL_WIDTH), 1.0),
        'page_table': page_table,
        'meta_tokens': nrm(ks[6], (N_META, D_MODEL), 1.0),
        'g_pre_mix': gain(ks[7]),
        'w_in': nrm(ks[8], (DEPTH, D_MODEL, IN_WIDTH), D_MODEL ** -0.5),
        'gate_bias': nrm(ks[9], (DEPTH, 2 * D_MODEL), 0.02),
        'sb_bias': SB_BIAS_INIT + nrm(ks[20], (DEPTH, SB_HEADS), 0.1),
        'w_pool': nrm(ks[10], (DEPTH, N_POOL_GROUPS, POOL_GROUP_DIM, POOL_GROUP_DIM), POOL_GROUP_DIM ** -0.5),
        'pool_scale': 1.0 + nrm(ks[11], (DEPTH, POOL_WIDTH), 0.02),
        'w_up_attn': nrm(ks[12], (DEPTH, SB_WIDTH, D_MODEL), SB_WIDTH ** -0.5),
        'w_up_pool': nrm(ks[13], (DEPTH, POOL_WIDTH, D_MODEL), POOL_WIDTH ** -0.5),
        'w_out': nrm(ks[14], (DEPTH, D_MODEL, D_MODEL), D_MODEL ** -0.5),
        'g_post_mix': gain(ks[15]),
        'g_pre_mlp': gain(ks[16]),
        'w_ff1': nrm(ks[17], (DEPTH, D_MODEL, D_FF), D_MODEL ** -0.5),
        'w_ff2': nrm(ks[18], (DEPTH, D_FF, D_MODEL), D_FF ** -0.5),
        'g_post_mlp': gain(ks[19]),
    }


def reference(x_prompt, x_sample, cache_k, cache_v, state_pool, page_table, meta_tokens,
              g_pre_mix, w_in, gate_bias, sb_bias, w_pool, pool_scale, w_up_attn, w_up_pool, w_out,
              g_post_mix, g_pre_mlp, w_ff1, w_ff2, g_post_mlp):
    b = x_prompt.shape[0]
    meta = jnp.broadcast_to(meta_tokens.astype(x_prompt.dtype)[None], (b, N_META, D_MODEL))
    xp = jnp.concatenate([meta, x_prompt], axis=1)
    xs = x_sample
    kp_l, vp_l, pp_l, ks_l, vs_l, ps_l = [], [], [], [], [], []
    for l in range(DEPTH):
        q, k, v, u, ga, gp = in_proj(rmsnorm(xp, g_pre_mix[l]), w_in[l], gate_bias[l])
        o_attn = prompt_attention(q, k, v, sb_bias[l])
        o_pool = pool_mix(u, w_pool[l], pool_scale[l])
        xp = merge_and_mlp(xp, o_attn, o_pool, ga, gp, w_up_attn[l], w_up_pool[l], w_out[l],
                           g_post_mix[l], g_pre_mlp[l], w_ff1[l], w_ff2[l], g_post_mlp[l])
        kp_l.append(k)
        vp_l.append(v)
        pp_l.append(u[:, -POOL_CTX:])
        q, k, v, u, ga, gp = in_proj(rmsnorm(xs, g_pre_mix[l]), w_in[l], gate_bias[l])
        o_attn = sample_attention(q, k, v, cache_k[l], cache_v[l], page_table, sb_bias[l])
        u_ext = jnp.concatenate([state_pool[l].astype(u.dtype), u], axis=1)
        o_pool = pool_mix(u_ext, w_pool[l], pool_scale[l])[:, POOL_CTX:]
        xs = merge_and_mlp(xs, o_attn, o_pool, ga, gp, w_up_attn[l], w_up_pool[l], w_out[l],
                           g_post_mix[l], g_pre_mlp[l], w_ff1[l], w_ff2[l], g_post_mlp[l])
        ks_l.append(k)
        vs_l.append(v)
        ps_l.append(u_ext[:, -POOL_CTX:])
    y_prompt = xp[:, N_META:]
    return (y_prompt, xs, jnp.stack(kp_l), jnp.stack(vp_l), jnp.stack(pp_l),
            jnp.stack(ks_l), jnp.stack(vs_l), jnp.stack(ps_l))
```

```python
import functools
import math

import jax
import jax.numpy as jnp
from jax import lax
from jax.experimental import pallas as pl
from jax.experimental.pallas import tpu as pltpu

HEAD_DIM = 128
PAGE = 128
BLOCK = 128
POOL_WINDOWS = (2, 4, 8, 16)
POOL_CTX = max(POOL_WINDOWS) - 1
POOL_HALO = 16
NORM_EPS = 1e-6
SB_SCALE = 1.0 / math.sqrt(HEAD_DIM)
VMEM_LIMIT = 56 * 1024 * 1024

F32 = jnp.float32
BF16 = jnp.bfloat16


def _params(sem, vmem=VMEM_LIMIT):
    return pltpu.CompilerParams(dimension_semantics=sem, vmem_limit_bytes=vmem)


def _pick(n, cap, align):
    best = None
    for d in range(align, min(n, cap) + 1, align):
        if n % d == 0:
            best = d
    assert best is not None, (n, cap, align)
    return best


def _neg_softplus(z):
    return -(jnp.maximum(z, 0.0) + jnp.log(1.0 + jnp.exp(-jnp.abs(z))))


def _split_bf16(x):
    hi = x.astype(BF16)
    lo = (x - hi.astype(F32)).astype(BF16)
    return hi, lo


def _suffix_matrix(n):
    r = lax.broadcasted_iota(jnp.int32, (n, 2 * n), 0)
    c = lax.broadcasted_iota(jnp.int32, (n, 2 * n), 1)
    return jnp.where((r > c) | (c >= n), 1.0, 0.0).astype(BF16)


def _rms(x, g):
    return x * lax.rsqrt(jnp.mean(x * x, axis=-1, keepdims=True) + NORM_EPS) * g


def _norm_cast_kernel(x_ref, g_ref, h_ref):
    h_ref[...] = _rms(x_ref[...], g_ref[...]).astype(h_ref.dtype)


def _norm_cast(x, g, tr):
    r, d = x.shape
    return pl.pallas_call(
        _norm_cast_kernel,
        grid=(r // tr,),
        in_specs=[pl.BlockSpec((tr, d), lambda i: (i, 0)),
                  pl.BlockSpec((1, d), lambda i: (0, 0))],
        out_specs=pl.BlockSpec((tr, d), lambda i: (i, 0)),
        out_shape=jax.ShapeDtypeStruct((r, d), BF16),
        compiler_params=_params(("parallel",)),
    )(x, g.reshape(1, d))


def _resid_norm_kernel(x_ref, y_ref, g_ref, o_ref):
    o_ref[...] = x_ref[...] + _rms(y_ref[...], g_ref[...])


def _resid_norm(x, y, g, tr):
    r, d = x.shape
    row = pl.BlockSpec((tr, d), lambda i: (i, 0))
    vec = pl.BlockSpec((1, d), lambda i: (0, 0))
    return pl.pallas_call(
        _resid_norm_kernel,
        grid=(r // tr,),
        in_specs=[row, row, vec],
        out_specs=row,
        out_shape=jax.ShapeDtypeStruct((r, d), F32),
        compiler_params=_params(("parallel",)),
    )(x, y, g.reshape(1, d))


def _resid_norm2_kernel(x_ref, y_ref, g_ref, g2_ref, o_ref, h_ref):
    x1 = x_ref[...] + _rms(y_ref[...], g_ref[...])
    o_ref[...] = x1
    h_ref[...] = _rms(x1, g2_ref[...]).astype(h_ref.dtype)


def _resid_norm2(x, y, g, g2, tr):
    r, d = x.shape
    row = pl.BlockSpec((tr, d), lambda i: (i, 0))
    vec = pl.BlockSpec((1, d), lambda i: (0, 0))
    return pl.pallas_call(
        _resid_norm2_kernel,
        grid=(r // tr,),
        in_specs=[row, row, vec, vec],
        out_specs=[row, row],
        out_shape=[jax.ShapeDtypeStruct((r, d), F32), jax.ShapeDtypeStruct((r, d), BF16)],
        compiler_params=_params(("parallel",)),
    )(x, y, g.reshape(1, d), g2.reshape(1, d))


def _mm_kernel(a_ref, w_ref, o_ref, *scratch, epilogue, nk):
    part = jnp.dot(a_ref[...], w_ref[...].astype(BF16), preferred_element_type=F32)

    def finish(acc):
        if epilogue == "relu2":
            acc = jnp.square(jnp.maximum(acc, 0.0))
        o_ref[...] = acc.astype(o_ref.dtype)

    if nk == 1:
        finish(part)
    else:
        acc_ref, = scratch
        k = pl.program_id(2)

        @pl.when(k == 0)
        def _():
            acc_ref[...] = part

        @pl.when((k > 0) & (k < nk - 1))
        def _():
            acc_ref[...] += part

        @pl.when(k == nk - 1)
        def _():
            finish(acc_ref[...] + part)


def _matmul(a, w, *, tm, tn, tk, out_dtype, epilogue=None):
    r, kdim = a.shape
    n = w.shape[1]
    nk = kdim // tk
    scratch = [] if nk == 1 else [pltpu.VMEM((tm, tn), F32)]
    return pl.pallas_call(
        functools.partial(_mm_kernel, epilogue=epilogue, nk=nk),
        grid=(r // tm, n // tn, nk),
        in_specs=[pl.BlockSpec((tm, tk), lambda i, j, k: (i, k)),
                  pl.BlockSpec((tk, tn), lambda i, j, k: (k, j))],
        out_specs=pl.BlockSpec((tm, tn), lambda i, j, k: (i, j)),
        out_shape=jax.ShapeDtypeStruct((r, n), out_dtype),
        scratch_shapes=scratch,
        compiler_params=_params(("parallel", "parallel", "arbitrary")),
    )(a, w)


def _in_proj_kernel(a_ref, w_ref, b_ref, o_ref, *, gate_tile0):
    acc = jnp.dot(a_ref[...], w_ref[...].astype(BF16), preferred_element_type=F32)
    j = pl.program_id(1)

    @pl.when(j < gate_tile0)
    def _():
        o_ref[...] = acc

    @pl.when(j >= gate_tile0)
    def _():
        o_ref[...] = 1.0 / (1.0 + jnp.exp(-(acc + b_ref[...])))


def _in_proj(h, w_in, gate_bias, *, tm, tn):
    r, d = h.shape
    n = w_in.shape[1]
    n_gate = gate_bias.shape[0]
    gate_tile0 = (n - n_gate) // tn
    return pl.pallas_call(
        functools.partial(_in_proj_kernel, gate_tile0=gate_tile0),
        grid=(r // tm, n // tn),
        in_specs=[pl.BlockSpec((tm, d), lambda i, j: (i, 0)),
                  pl.BlockSpec((d, tn), lambda i, j: (0, j)),
                  pl.BlockSpec((1, tn), lambda i, j: (0, jnp.maximum(j - gate_tile0, 0)))],
        out_specs=pl.BlockSpec((tm, tn), lambda i, j: (i, j)),
        out_shape=jax.ShapeDtypeStruct((r, n), F32),
        compiler_params=_params(("parallel", "parallel")),
    )(h, w_in, gate_bias.reshape(1, n_gate))


def _up_gate_kernel(oa_ref, op_ref, wa_ref, wp_ref, ga_ref, gp_ref, o_ref):
    ya = jnp.dot(oa_ref[...], wa_ref[...].astype(BF16), preferred_element_type=F32)
    yp = jnp.dot(op_ref[...], wp_ref[...].astype(BF16), preferred_element_type=F32)
    o_ref[...] = (ga_ref[...] * ya + gp_ref[...] * yp).astype(o_ref.dtype)


def _up_gate(o_attn, o_pool, w_up_attn, w_up_pool, p, gate_col0, *, tm, tn):
    r, c = o_attn.shape
    d = w_up_attn.shape[1]
    ga0 = gate_col0 // tn
    gp0 = (gate_col0 + d) // tn
    return pl.pallas_call(
        _up_gate_kernel,
        grid=(r // tm, d // tn),
        in_specs=[pl.BlockSpec((tm, c), lambda i, j: (i, 0)),
                  pl.BlockSpec((tm, c), lambda i, j: (i, 0)),
                  pl.BlockSpec((c, tn), lambda i, j: (0, j)),
                  pl.BlockSpec((c, tn), lambda i, j: (0, j)),
                  pl.BlockSpec((tm, tn), lambda i, j: (i, ga0 + j)),
                  pl.BlockSpec((tm, tn), lambda i, j: (i, gp0 + j))],
        out_specs=pl.BlockSpec((tm, tn), lambda i, j: (i, j)),
        out_shape=jax.ShapeDtypeStruct((r, d), BF16),
        compiler_params=_params(("parallel", "parallel")),
    )(o_attn, o_pool, w_up_attn, w_up_pool, p, p)


def _sb_block(s, valid, cb, uo):
    n = s.shape[1]
    lf = _neg_softplus(s)
    if valid is not None:
        lf = jnp.where(valid, lf, 0.0)
    hi, lo = _split_bf16(lf)
    r = (jnp.dot(hi, uo, preferred_element_type=F32)
         + jnp.dot(lo, uo, preferred_element_type=F32))
    a = jnp.exp(s + lf + cb + r[:, :n])
    if valid is not None:
        a = jnp.where(valid, a, 0.0)
    return a, cb + r[:, n:]


def _attn_prompt_kernel(bias_ref, q_ref, k_ref, v_ref, uo_ref, o_ref, *, front):
    h = pl.program_id(0)
    i = pl.program_id(1)
    bias = bias_ref[h]
    q = (q_ref[...] * SB_SCALE).astype(BF16)
    uo = uo_ref[...]
    row = lax.broadcasted_iota(jnp.int32, (BLOCK, BLOCK), 0)
    col = lax.broadcasted_iota(jnp.int32, (BLOCK, BLOCK), 1)

    def body(t, carry):
        acc, cb = carry
        kb = i - t
        start = pl.multiple_of(kb * BLOCK, BLOCK)
        k = k_ref[pl.ds(start, BLOCK), :].astype(BF16)
        v = v_ref[pl.ds(start, BLOCK), :].astype(BF16)
        s = lax.dot_general(q, k, (((1,), (1,)), ((), ())), preferred_element_type=F32) + bias
        key_row = start + col
        valid = (key_row < i * BLOCK + row) & (key_row >= front)
        a, cb = _sb_block(s, valid, cb, uo)
        acc = acc + jnp.dot(a.astype(BF16), v, preferred_element_type=F32)
        return acc, cb

    zeros = jnp.zeros((BLOCK, BLOCK), F32)
    acc, _ = lax.fori_loop(0, i + 1, body, (zeros, zeros))
    o_ref[...] = acc.astype(o_ref.dtype)


def _attn_prompt(p, sb_bias, *, rows, heads, front):
    nb = rows // BLOCK
    return pl.pallas_call(
        functools.partial(_attn_prompt_kernel, front=front),
        grid=(heads, nb),
        in_specs=[pl.BlockSpec(memory_space=pltpu.SMEM),
                  pl.BlockSpec((BLOCK, HEAD_DIM), lambda h, i: (i, h)),
                  pl.BlockSpec((rows, HEAD_DIM), lambda h, i: (0, heads + h)),
                  pl.BlockSpec((rows, HEAD_DIM), lambda h, i: (0, 2 * heads + h)),
                  pl.BlockSpec((BLOCK, 2 * BLOCK), lambda h, i: (0, 0))],
        out_specs=pl.BlockSpec((BLOCK, HEAD_DIM), lambda h, i: (i, h)),
        out_shape=jax.ShapeDtypeStruct((rows, heads * HEAD_DIM), BF16),
        compiler_params=_params(("parallel", "arbitrary")),
    )(sb_bias, p, p, p, _suffix_matrix(BLOCK))


def _attn_sample_kernel(pt_ref, q_ref, bias_ref, kn_ref, vn_ref, kp_ref, vp_ref, uo_ref, o_ref,
                        acc_ref, cb_ref, kbuf_ref, vbuf_ref, *, heads, tq):
    del pt_ref
    j = pl.program_id(1)
    uo = uo_ref[...]
    bias = bias_ref[...]

    def block(k_ref, v_ref, valid):
        s = jnp.concatenate(
            [lax.dot_general(q_ref[hh * tq:(hh + 1) * tq, :].astype(BF16),
                             k_ref[:, hh * HEAD_DIM:(hh + 1) * HEAD_DIM].astype(BF16),
                             (((1,), (1,)), ((), ())), preferred_element_type=F32)
             for hh in range(heads)], axis=0) + bias
        a, cb = _sb_block(s, valid, cb_ref[...], uo)
        cb_ref[...] = cb
        for hh in range(heads):
            acc_ref[hh * tq:(hh + 1) * tq, :] += jnp.dot(
                a[hh * tq:(hh + 1) * tq, :].astype(BF16),
                v_ref[:, hh * HEAD_DIM:(hh + 1) * HEAD_DIM].astype(BF16),
                preferred_element_type=F32)

    @pl.when(j == 0)
    def _():
        acc_ref[...] = jnp.zeros_like(acc_ref)
        cb_ref[...] = jnp.zeros_like(cb_ref)
        kbuf_ref[...] = jnp.zeros_like(kbuf_ref)
        vbuf_ref[...] = jnp.zeros_like(vbuf_ref)
        kbuf_ref[0:tq, :] = kn_ref[...]
        vbuf_ref[0:tq, :] = vn_ref[...]
        row = lax.broadcasted_iota(jnp.int32, (heads * tq, PAGE), 0)
        col = lax.broadcasted_iota(jnp.int32, (heads * tq, PAGE), 1)
        block(kbuf_ref, vbuf_ref, col < (row & (tq - 1)))

    block(kp_ref, vp_ref, None)

    @pl.when(j == pl.num_programs(1) - 1)
    def _():
        o_ref[...] = acc_ref[...]


def _attn_sample(q_s, bias_rows, k_new, v_new, cache_k, cache_v, page_table, *, heads, tq):
    db, n_pages = page_table.shape
    width = heads * HEAD_DIM
    rows = heads * tq

    def page_map(b, j, pt):
        return (pt[b * n_pages + (n_pages - 1 - j)], 0, 0)

    grid_spec = pltpu.PrefetchScalarGridSpec(
        num_scalar_prefetch=1,
        grid=(db, n_pages),
        in_specs=[pl.BlockSpec((None, rows, HEAD_DIM), lambda b, j, pt: (b, 0, 0)),
                  pl.BlockSpec((rows, PAGE), lambda b, j, pt: (0, 0)),
                  pl.BlockSpec((None, tq, width), lambda b, j, pt: (b, 0, 0)),
                  pl.BlockSpec((None, tq, width), lambda b, j, pt: (b, 0, 0)),
                  pl.BlockSpec((None, PAGE, width), page_map),
                  pl.BlockSpec((None, PAGE, width), page_map),
                  pl.BlockSpec((PAGE, 2 * PAGE), lambda b, j, pt: (0, 0))],
        out_specs=pl.BlockSpec((None, rows, HEAD_DIM), lambda b, j, pt: (b, 0, 0)),
        scratch_shapes=[pltpu.VMEM((rows, HEAD_DIM), F32),
                        pltpu.VMEM((rows, PAGE), F32),
                        pltpu.VMEM((PAGE, width), F32),
                        pltpu.VMEM((PAGE, width), F32)],
    )
    return pl.pallas_call(
        functools.partial(_attn_sample_kernel, heads=heads, tq=tq),
        grid_spec=grid_spec,
        out_shape=jax.ShapeDtypeStruct((db, rows, HEAD_DIM), F32),
        compiler_params=_params(("parallel", "arbitrary")),
    )(page_table.reshape(-1), q_s, bias_rows, k_new, v_new, cache_k, cache_v, _suffix_matrix(PAGE))


def _pool_prompt_kernel(u_ref, halo_ref, w_ref, sc_ref, o_ref, ext_ref, *, front, tr, gdim):
    i = pl.program_id(0)
    halo = halo_ref[...]
    ext_ref[0:POOL_HALO, :] = jnp.where(i > 0, halo, 0.0)
    ext_ref[POOL_HALO:, :] = u_ref[...]
    pos = i * tr + lax.broadcasted_iota(jnp.int32, (tr, gdim), 0) - front
    for g, w in enumerate(POOL_WINDOWS):
        cols = slice(g * gdim, (g + 1) * gdim)
        u = ext_ref[POOL_HALO:, cols]
        win = u
        for k in range(1, w):
            win = win + ext_ref[POOL_HALO - k:POOL_HALO - k + tr, cols]
        cnt = jnp.clip(pos + 1, 1, w).astype(F32)
        pooled = (win / cnt - u).astype(BF16)
        y = jnp.dot(pooled, w_ref[g].astype(BF16), preferred_element_type=F32)
        o_ref[:, cols] = (y * sc_ref[:, cols]).astype(o_ref.dtype)


def _pool_prompt(p, w_pool, pool_scale, *, rows, u_col0, front, tr):
    ng, gdim, _ = w_pool.shape
    width = ng * gdim
    cb = u_col0 // width
    hb = tr // POOL_HALO
    return pl.pallas_call(
        functools.partial(_pool_prompt_kernel, front=front, tr=tr, gdim=gdim),
        grid=(rows // tr,),
        in_specs=[pl.BlockSpec((tr, width), lambda i: (i, cb)),
                  pl.BlockSpec((POOL_HALO, width), lambda i: (jnp.maximum(i * hb - 1, 0), cb)),
                  pl.BlockSpec((ng, gdim, gdim), lambda i: (0, 0, 0)),
                  pl.BlockSpec((1, width), lambda i: (0, 0))],
        out_specs=pl.BlockSpec((tr, width), lambda i: (i, 0)),
        out_shape=jax.ShapeDtypeStruct((rows, width), BF16),
        scratch_shapes=[pltpu.VMEM((POOL_HALO + tr, width), F32)],
        compiler_params=_params(("parallel",)),
    )(p, p, w_pool, pool_scale.reshape(1, width))


def _pool_sample_kernel(st_ref, u_ref, w_ref, sc_ref, o_ref, *, gdim, n_new):
    ctx = st_ref.shape[0]
    rows = [st_ref[r] for r in range(ctx)] + [u_ref[r] for r in range(n_new)]
    for g, w in enumerate(POOL_WINDOWS):
        cols = slice(g * gdim, (g + 1) * gdim)
        wg = w_ref[g].astype(BF16)
        for s in range(n_new):
            last = ctx + s
            win = rows[last][:, cols]
            for k in range(1, w):
                win = win + rows[last - k][:, cols]
            pooled = (win / float(w) - rows[last][:, cols]).astype(BF16)
            y = jnp.dot(pooled, wg, preferred_element_type=F32)
            o_ref[s, :, cols] = (y * sc_ref[:, cols]).astype(o_ref.dtype)


def _pool_sample(state_t, u_t, w_pool, pool_scale, *, tb):
    ctx, db, width = state_t.shape
    n_new = u_t.shape[0]
    ng, gdim, _ = w_pool.shape
    assert ctx >= max(POOL_WINDOWS) - 1
    return pl.pallas_call(
        functools.partial(_pool_sample_kernel, gdim=gdim, n_new=n_new),
        grid=(db // tb,),
        in_specs=[pl.BlockSpec((ctx, tb, width), lambda i: (0, i, 0)),
                  pl.BlockSpec((n_new, tb, width), lambda i: (0, i, 0)),
                  pl.BlockSpec((ng, gdim, gdim), lambda i: (0, 0, 0)),
                  pl.BlockSpec((1, width), lambda i: (0, 0))],
        out_specs=pl.BlockSpec((n_new, tb, width), lambda i: (0, i, 0)),
        out_shape=jax.ShapeDtypeStruct((n_new, db, width), BF16),
        compiler_params=_params(("parallel",)),
    )(state_t, u_t, w_pool, pool_scale.reshape(1, width))


def kernel(x_prompt, x_sample, cache_k, cache_v, state_pool, page_table, meta_tokens, g_pre_mix, w_in, gate_bias, sb_bias, w_pool, pool_scale, w_up_attn, w_up_pool, w_out, g_post_mix, g_pre_mlp, w_ff1, w_ff2, g_post_mlp):
    batch, seq, d = x_prompt.shape
    db, ds, _ = x_sample.shape
    depth = w_in.shape[0]
    n_meta = meta_tokens.shape[0]
    heads = cache_k.shape[3]
    sbw = heads * HEAD_DIM
    ng, gdim = w_pool.shape[1], w_pool.shape[2]
    pw = ng * gdim
    assert batch == 1 and depth == 1, "one prompt sequence, one layer"
    assert cache_k.shape[2] == PAGE and cache_k.shape[4] == HEAD_DIM
    assert seq % BLOCK == 0 and n_meta <= BLOCK and ds <= 8
    assert state_pool.shape[2] == POOL_CTX and sbw == pw

    front = BLOCK - n_meta
    rows_p = BLOCK + seq
    n_s = db * ds
    rows = rows_p + n_s
    q0, k0, v0, u0, g0 = 0, sbw, 2 * sbw, 3 * sbw, 3 * sbw + pw

    x_all = jnp.concatenate([jnp.zeros((front, d), F32), meta_tokens.astype(F32),
                             x_prompt[0], x_sample.reshape(n_s, d)], axis=0)

    tm = _pick(rows, 1200, 16)
    te = _pick(rows, 400, 16)
    tn = 512

    h0 = _norm_cast(x_all, g_pre_mix[0], te)
    p = _in_proj(h0, w_in[0], gate_bias[0], tm=tm, tn=tn)

    o_attn_p = _attn_prompt(p, sb_bias[0], rows=rows_p, heads=heads, front=front)

    tq = 8
    p_s = p[rows_p:]
    q_s = (p_s[:, q0:q0 + sbw] * SB_SCALE).reshape(db, ds, heads, HEAD_DIM)
    q_s = jnp.pad(q_s.transpose(0, 2, 1, 3), ((0, 0), (0, 0), (0, tq - ds), (0, 0)))
    q_s = q_s.reshape(db, heads * tq, HEAD_DIM)
    k_s = p_s[:, k0:k0 + sbw].reshape(db, ds, sbw)
    v_s = p_s[:, v0:v0 + sbw].reshape(db, ds, sbw)
    pad_new = ((0, 0), (0, tq - ds), (0, 0))
    bias_rows = jnp.broadcast_to(jnp.repeat(sb_bias[0].astype(F32), tq)[:, None], (heads * tq, PAGE))
    n_phys = cache_k.shape[1]
    o_s = _attn_sample(q_s, bias_rows, jnp.pad(k_s, pad_new), jnp.pad(v_s, pad_new),
                       cache_k.reshape(n_phys, PAGE, sbw), cache_v.reshape(n_phys, PAGE, sbw),
                       page_table, heads=heads, tq=tq)
    o_attn_s = o_s.reshape(db, heads, tq, HEAD_DIM)[:, :, :ds].transpose(0, 2, 1, 3)
    o_attn = jnp.concatenate([o_attn_p, o_attn_s.reshape(n_s, sbw).astype(BF16)], axis=0)

    o_pool_p = _pool_prompt(p, w_pool[0], pool_scale[0], rows=rows_p, u_col0=u0, front=front,
                            tr=_pick(rows_p, 700, 16))
    u_s = p_s[:, u0:u0 + pw].reshape(db, ds, pw)
    o_pool_s = _pool_sample(state_pool[0].transpose(1, 0, 2), u_s.transpose(1, 0, 2),
                            w_pool[0], pool_scale[0], tb=_pick(db, 32, 8))
    o_pool = jnp.concatenate([o_pool_p, o_pool_s.transpose(1, 0, 2).reshape(n_s, pw)], axis=0)

    m = _up_gate(o_attn, o_pool, w_up_attn[0], w_up_pool[0], p, g0, tm=tm, tn=tn // 2)
    mix = _matmul(m, w_out[0], tm=tm, tn=tn, tk=d, out_dtype=F32)
    x1, h2 = _resid_norm2(x_all, mix, g_post_mix[0], g_pre_mlp[0], te)

    dff = w_ff1.shape[2]
    hid = _matmul(h2, w_ff1[0], tm=tm, tn=tn, tk=d, out_dtype=BF16, epilogue="relu2")
    ffo = _matmul(hid, w_ff2[0], tm=tm, tn=2 * tn, tk=_pick(dff, 2048, 128), out_dtype=F32)
    y = _resid_norm(x1, ffo, g_post_mlp[0], te)

    y_prompt = y[BLOCK:rows_p][None]
    y_sample = y[rows_p:].reshape(db, ds, d)
    kv_shape = (1, 1, n_meta + seq, heads, HEAD_DIM)
    k_prompt = p[front:rows_p, k0:k0 + sbw].reshape(kv_shape)
    v_prompt = p[front:rows_p, v0:v0 + sbw].reshape(kv_shape)
    pool_prompt = p[rows_p - POOL_CTX:rows_p, u0:u0 + pw][None, None]
    k_sample = k_s.reshape(1, db, ds, heads, HEAD_DIM)
    v_sample = v_s.reshape(1, db, ds, heads, HEAD_DIM)
    pool_sample = jnp.concatenate([state_pool[0, :, ds:], u_s], axis=1)[None]
    return (y_prompt, y_sample, k_prompt, v_prompt, pool_prompt, k_sample, v_sample, pool_sample)
```

```python
import functools
import math

import jax
import jax.numpy as jnp
from jax import lax
from jax.experimental import pallas as pl
from jax.experimental.pallas import tpu as pltpu

HEAD_DIM = 128
PAGE = 128
ABLK = 256
HEAD_GROUP = 4
PAGES_PER_STEP = 4
POOL_WINDOWS = (2, 4, 8, 16)
POOL_CTX = max(POOL_WINDOWS) - 1
POOL_HALO = 16
NORM_EPS = 1e-6
LOG2E = 1.4426950408889634
SB_SCALE2 = LOG2E / math.sqrt(HEAD_DIM)
MASKED = -1e30
VMEM_LIMIT = 56 * 1024 * 1024

F32 = jnp.float32
BF16 = jnp.bfloat16
NT_DIMS = (((1,), (1,)), ((), ()))


def _params(sem, vmem=VMEM_LIMIT):
    return pltpu.CompilerParams(dimension_semantics=sem, vmem_limit_bytes=vmem)


def _pick(n, cap, align):
    best = None
    for d in range(align, min(n, cap) + 1, align):
        if n % d == 0:
            best = d
    assert best is not None, (n, cap, align)
    return best


def _log2_fail(z):
    nz = -z
    e = jnp.exp2(jnp.minimum(z, nz))
    return jnp.minimum(nz, 0.0) - jnp.log(1.0 + e) * LOG2E


def _rms(x, g):
    return x * lax.rsqrt(jnp.mean(x * x, axis=-1, keepdims=True) + NORM_EPS) * g


def _norm_cast_kernel(x_ref, g_ref, h_ref):
    h_ref[...] = _rms(x_ref[...], g_ref[...]).astype(h_ref.dtype)


def _norm_cast(x, g, tr):
    r, d = x.shape
    return pl.pallas_call(
        _norm_cast_kernel,
        grid=(r // tr,),
        in_specs=[pl.BlockSpec((tr, d), lambda i: (i, 0)),
                  pl.BlockSpec((1, d), lambda i: (0, 0))],
        out_specs=pl.BlockSpec((tr, d), lambda i: (i, 0)),
        out_shape=jax.ShapeDtypeStruct((r, d), BF16),
        compiler_params=_params(("parallel",)),
    )(x, g.reshape(1, d))


def _resid_norm_kernel(x_ref, y_ref, g_ref, o_ref):
    o_ref[...] = x_ref[...] + _rms(y_ref[...], g_ref[...])


def _resid_norm(x, y, g, tr):
    r, d = x.shape
    row = pl.BlockSpec((tr, d), lambda i: (i, 0))
    vec = pl.BlockSpec((1, d), lambda i: (0, 0))
    return pl.pallas_call(
        _resid_norm_kernel,
        grid=(r // tr,),
        in_specs=[row, row, vec],
        out_specs=row,
        out_shape=jax.ShapeDtypeStruct((r, d), F32),
        compiler_params=_params(("parallel",)),
    )(x, y, g.reshape(1, d))


def _resid_norm2_kernel(x_ref, y_ref, g_ref, g2_ref, o_ref, h_ref):
    x1 = x_ref[...] + _rms(y_ref[...], g_ref[...])
    o_ref[...] = x1
    h_ref[...] = _rms(x1, g2_ref[...]).astype(h_ref.dtype)


def _resid_norm2(x, y, g, g2, tr):
    r, d = x.shape
    row = pl.BlockSpec((tr, d), lambda i: (i, 0))
    vec = pl.BlockSpec((1, d), lambda i: (0, 0))
    return pl.pallas_call(
        _resid_norm2_kernel,
        grid=(r // tr,),
        in_specs=[row, row, vec, vec],
        out_specs=[row, row],
        out_shape=[jax.ShapeDtypeStruct((r, d), F32), jax.ShapeDtypeStruct((r, d), BF16)],
        compiler_params=_params(("parallel",)),
    )(x, y, g.reshape(1, d), g2.reshape(1, d))


def _column_halves(width):
    half = width // 2
    return [slice(0, half), slice(half, width)]


def _mm_kernel(a_ref, w_ref, o_ref, *, epilogue, nk):
    if nk > 1:
        @pl.when(pl.program_id(2) == 0)
        def _():
            o_ref[...] = jnp.zeros_like(o_ref)

    for cols in _column_halves(o_ref.shape[1]):
        part = jnp.dot(a_ref[...], w_ref[:, cols].astype(BF16), preferred_element_type=F32)
        if nk > 1:
            o_ref[:, cols] += part
        else:
            if epilogue == "relu2":
                part = jnp.square(jnp.maximum(part, 0.0))
            o_ref[:, cols] = part.astype(o_ref.dtype)


def _matmul(a, w, *, tm, tn, tk, out_dtype, epilogue=None):
    r, kdim = a.shape
    n = w.shape[1]
    nk = kdim // tk
    assert nk == 1 or (out_dtype == F32 and epilogue is None)
    return pl.pallas_call(
        functools.partial(_mm_kernel, epilogue=epilogue, nk=nk),
        grid=(r // tm, n // tn, nk),
        in_specs=[pl.BlockSpec((tm, tk), lambda i, j, k: (i, k)),
                  pl.BlockSpec((tk, tn), lambda i, j, k: (k, j))],
        out_specs=pl.BlockSpec((tm, tn), lambda i, j, k: (i, j)),
        out_shape=jax.ShapeDtypeStruct((r, n), out_dtype),
        compiler_params=_params(("parallel", "parallel", "arbitrary")),
    )(a, w)


def _in_proj_kernel(a_ref, w_ref, b_ref, o_ref, o16_ref, *, q_tiles, qkv_tiles, gate_tile0):
    for cols in _column_halves(o_ref.shape[1]):
        o_ref[:, cols] = jnp.dot(a_ref[...], w_ref[:, cols].astype(BF16),
                                 preferred_element_type=F32)
    j = pl.program_id(1)

    @pl.when(j < q_tiles)
    def _():
        o16_ref[...] = (o_ref[...] * SB_SCALE2).astype(BF16)

    @pl.when((j >= q_tiles) & (j < qkv_tiles))
    def _():
        o16_ref[...] = o_ref[...].astype(BF16)

    @pl.when(j >= gate_tile0)
    def _():
        o_ref[...] = 1.0 / (1.0 + jnp.exp(-(o_ref[...] + b_ref[...])))


def _in_proj(h, w_in, gate_bias, *, tm, tn, q_cols, qkv_cols):
    r, d = h.shape
    n = w_in.shape[1]
    n_gate = gate_bias.shape[0]
    gate_tile0 = (n - n_gate) // tn
    q_tiles, qkv_tiles = q_cols // tn, qkv_cols // tn
    return pl.pallas_call(
        functools.partial(_in_proj_kernel, q_tiles=q_tiles, qkv_tiles=qkv_tiles,
                          gate_tile0=gate_tile0),
        grid=(r // tm, n // tn),
        in_specs=[pl.BlockSpec((tm, d), lambda i, j: (i, 0)),
                  pl.BlockSpec((d, tn), lambda i, j: (0, j)),
                  pl.BlockSpec((1, tn), lambda i, j: (0, jnp.maximum(j - gate_tile0, 0)))],
        out_specs=[pl.BlockSpec((tm, tn), lambda i, j: (i, j)),
                   pl.BlockSpec((tm, tn), lambda i, j: (i, jnp.minimum(j, qkv_tiles - 1)))],
        out_shape=[jax.ShapeDtypeStruct((r, n), F32),
                   jax.ShapeDtypeStruct((r, qkv_cols), BF16)],
        compiler_params=_params(("parallel", "arbitrary")),
    )(h, w_in, gate_bias.reshape(1, n_gate))


def _up_gate_kernel(oa_ref, op_ref, wa_ref, wp_ref, ga_ref, gp_ref, o_ref):
    ya = jnp.dot(oa_ref[...], wa_ref[...].astype(BF16), preferred_element_type=F32)
    yp = jnp.dot(op_ref[...], wp_ref[...].astype(BF16), preferred_element_type=F32)
    o_ref[...] = (ga_ref[...] * ya + gp_ref[...] * yp).astype(o_ref.dtype)


def _up_gate(o_attn, o_pool, w_up_attn, w_up_pool, p, gate_col0, *, tm, tn):
    r, c = o_attn.shape
    d = w_up_attn.shape[1]
    ga0 = gate_col0 // tn
    gp0 = (gate_col0 + d) // tn
    return pl.pallas_call(
        _up_gate_kernel,
        grid=(r // tm, d // tn),
        in_specs=[pl.BlockSpec((tm, c), lambda i, j: (i, 0)),
                  pl.BlockSpec((tm, c), lambda i, j: (i, 0)),
                  pl.BlockSpec((c, tn), lambda i, j: (0, j)),
                  pl.BlockSpec((c, tn), lambda i, j: (0, j)),
                  pl.BlockSpec((tm, tn), lambda i, j: (i, ga0 + j)),
                  pl.BlockSpec((tm, tn), lambda i, j: (i, gp0 + j))],
        out_specs=pl.BlockSpec((tm, tn), lambda i, j: (i, j)),
        out_shape=jax.ShapeDtypeStruct((r, d), BF16),
        compiler_params=_params(("parallel", "parallel")),
    )(o_attn, o_pool, w_up_attn, w_up_pool, p, p)


def _attn_prompt_kernel(bias_ref, q_ref, k_ref, v_ref, tri_ref, dmask_ref, o_ref,
                        vt_ref, z_ref, a_ref, acc_ref, *, nblk):
    hg = pl.program_id(0)
    i = pl.program_id(1)
    lanes = [slice(g * HEAD_DIM, (g + 1) * HEAD_DIM) for g in range(HEAD_GROUP)]

    @pl.when(i == 0)
    def _():
        def transpose_block(c, carry):
            st = pl.multiple_of(c * ABLK, ABLK)
            for g in range(HEAD_GROUP):
                vt_ref[g, :, pl.ds(st, ABLK)] = (
                    v_ref[pl.ds(st, ABLK), lanes[g]].astype(F32).T.astype(BF16))
            return carry
        lax.fori_loop(0, nblk, transpose_block, 0)

    bias2 = [bias_ref[hg * HEAD_GROUP + g] * LOG2E for g in range(HEAD_GROUP)]

    heads = range(HEAD_GROUP)

    def scores(kb):
        st = pl.multiple_of(kb * ABLK, ABLK)
        return tuple(lax.dot_general(k_ref[pl.ds(st, ABLK), lanes[g]], q_ref[:, lanes[g]],
                                     NT_DIMS, preferred_element_type=F32) + bias2[g]
                     for g in heads)

    def weights(z, cb):
        lf = [_log2_fail(zg) for zg in z]
        later = [jnp.dot(tri_ref[...], lfg.astype(BF16), preferred_element_type=F32)
                 for lfg in lf]
        a = tuple(jnp.exp2(z[g] + lf[g] + (cb[g] + later[g])).astype(BF16) for g in heads)
        cb = tuple(cb[g] + (later[g][0:1, :] + lf[g][0:1, :]) for g in heads)
        return a, cb

    def attend(kb):
        st = pl.multiple_of(kb * ABLK, ABLK)
        for g in heads:
            acc_ref[g] += jnp.dot(vt_ref[g, :, pl.ds(st, ABLK)], a_ref[g],
                                  preferred_element_type=F32)

    def stash(z, a):
        for g in heads:
            z_ref[g] = z[g]
            a_ref[g] = a[g]

    acc_ref[...] = jnp.zeros_like(acc_ref)
    cb = (jnp.zeros((1, ABLK), F32),) * HEAD_GROUP
    a, cb = weights([zg + dmask_ref[...] for zg in scores(i)], cb)
    stash(scores(jnp.maximum(i - 1, 0)), a)

    def body(t, cb):
        kb = i - 1 - t
        attend(kb + 1)
        z_next = scores(jnp.maximum(kb - 1, 0))
        a, cb = weights([z_ref[g] for g in heads], cb)
        stash(z_next, a)
        return cb

    lax.fori_loop(0, i, body, cb)
    attend(0)
    for g in heads:
        o_ref[:, lanes[g]] = acc_ref[g].T.astype(o_ref.dtype)


def _attn_prompt(qkv, sb_bias, *, rows, heads):
    nblk = rows // ABLK
    ngrp = heads // HEAD_GROUP
    gw = HEAD_GROUP * HEAD_DIM
    r = lax.broadcasted_iota(jnp.int32, (ABLK, ABLK), 0)
    c = lax.broadcasted_iota(jnp.int32, (ABLK, ABLK), 1)
    tri = jnp.where(c > r, 1.0, 0.0).astype(BF16)
    dmask = jnp.where(r < c, 0.0, MASKED).astype(F32)
    return pl.pallas_call(
        functools.partial(_attn_prompt_kernel, nblk=nblk),
        grid=(ngrp, nblk),
        in_specs=[pl.BlockSpec(memory_space=pltpu.SMEM),
                  pl.BlockSpec((ABLK, gw), lambda h, i: (i, h)),
                  pl.BlockSpec((rows, gw), lambda h, i: (0, ngrp + h)),
                  pl.BlockSpec((rows, gw), lambda h, i: (0, 2 * ngrp + h)),
                  pl.BlockSpec((ABLK, ABLK), lambda h, i: (0, 0)),
                  pl.BlockSpec((ABLK, ABLK), lambda h, i: (0, 0))],
        out_specs=pl.BlockSpec((ABLK, gw), lambda h, i: (i, h)),
        out_shape=jax.ShapeDtypeStruct((rows, heads * HEAD_DIM), BF16),
        scratch_shapes=[pltpu.VMEM((HEAD_GROUP, HEAD_DIM, rows), BF16),
                        pltpu.VMEM((HEAD_GROUP, ABLK, ABLK), F32),
                        pltpu.VMEM((HEAD_GROUP, ABLK, ABLK), BF16),
                        pltpu.VMEM((HEAD_GROUP, HEAD_DIM, ABLK), F32)],
        compiler_params=_params(("parallel", "arbitrary")),
    )(sb_bias, qkv, qkv, qkv, tri, dmask)


def _attn_sample_kernel(pt_ref, q_ref, bias_ref, kn_ref, vn_ref, *rest, heads, tq, n_new):
    npg = PAGES_PER_STEP
    kp_refs, vp_refs = rest[:npg], rest[npg:2 * npg]
    uo_ref, o_ref, acc_ref, cb_ref, kbuf_ref, vbuf_ref = rest[2 * npg:]
    del pt_ref
    j = pl.program_id(1)
    uo = uo_ref[...]
    bias2 = bias_ref[...]

    def by_head(ref):
        x = ref[...].reshape(PAGE, heads, HEAD_DIM)
        return pltpu.einshape("khd->hkd", x)

    def block(k_ref, v_ref, cb, outs, mask):
        kh = by_head(k_ref)
        vh = by_head(v_ref)
        z = jnp.concatenate(
            [lax.dot_general(q_ref[hh * tq:(hh + 1) * tq, :].astype(BF16),
                             kh[hh].astype(BF16),
                             NT_DIMS, preferred_element_type=F32)
             for hh in range(heads)], axis=0) + bias2
        if mask is not None:
            z = z + mask
        lf = _log2_fail(z)
        r = jnp.dot(lf.astype(BF16), uo, preferred_element_type=F32)
        a = jnp.exp2(z + lf + (cb + r[:, :PAGE]))
        outs = [o + jnp.dot(a[hh * tq:(hh + 1) * tq, :].astype(BF16),
                            vh[hh].astype(BF16),
                            preferred_element_type=F32)
                for hh, o in enumerate(outs)]
        return cb + r[:, PAGE:], outs

    @pl.when(j == 0)
    def _():
        kbuf_ref[...] = jnp.zeros_like(kbuf_ref)
        vbuf_ref[...] = jnp.zeros_like(vbuf_ref)
        kbuf_ref[0:n_new * heads, :] = kn_ref[...]
        vbuf_ref[0:n_new * heads, :] = vn_ref[...]
        row = lax.broadcasted_iota(jnp.int32, (heads * tq, PAGE), 0)
        col = lax.broadcasted_iota(jnp.int32, (heads * tq, PAGE), 1)
        mask = jnp.where(col < (row & (tq - 1)), 0.0, MASKED)
        cb, outs = block(kbuf_ref, vbuf_ref, jnp.zeros((heads * tq, PAGE), F32),
                         [jnp.zeros((tq, HEAD_DIM), F32)] * heads, mask)
        cb_ref[...] = cb
        acc_ref[...] = jnp.concatenate(outs, axis=0)

    cb = cb_ref[...]
    outs = [jnp.zeros((tq, HEAD_DIM), F32)] * heads
    for c in range(npg):
        cb, outs = block(kp_refs[c], vp_refs[c], cb, outs, None)
    cb_ref[...] = cb
    acc_ref[...] += jnp.concatenate(outs, axis=0)

    @pl.when(j == pl.num_programs(1) - 1)
    def _():
        o_ref[...] = acc_ref[...]


def _attn_sample(q_s, bias_rows, k_new, v_new, cache_k, cache_v, page_table, *, heads, tq):
    db, n_pages = page_table.shape
    rows = heads * tq
    npg = PAGES_PER_STEP
    assert n_pages % npg == 0 and tq & (tq - 1) == 0
    n_new = k_new.shape[1] // heads

    def page_map(c):
        return lambda b, j, pt: (pt[b * n_pages + (n_pages - 1 - (j * npg + c))], 0, 0)

    page_specs = [pl.BlockSpec((None, PAGE * heads, HEAD_DIM), page_map(c)) for c in range(npg)]
    uo_r = lax.broadcasted_iota(jnp.int32, (PAGE, 2 * PAGE), 0)
    uo_c = lax.broadcasted_iota(jnp.int32, (PAGE, 2 * PAGE), 1)
    uo = jnp.where((uo_r > uo_c) | (uo_c >= PAGE), 1.0, 0.0).astype(BF16)
    grid_spec = pltpu.PrefetchScalarGridSpec(
        num_scalar_prefetch=1,
        grid=(db, n_pages // npg),
        in_specs=[pl.BlockSpec((None, rows, HEAD_DIM), lambda b, j, pt: (b, 0, 0)),
                  pl.BlockSpec((rows, PAGE), lambda b, j, pt: (0, 0)),
                  pl.BlockSpec((None, n_new * heads, HEAD_DIM), lambda b, j, pt: (b, 0, 0)),
                  pl.BlockSpec((None, n_new * heads, HEAD_DIM), lambda b, j, pt: (b, 0, 0))]
                 + page_specs + page_specs
                 + [pl.BlockSpec((PAGE, 2 * PAGE), lambda b, j, pt: (0, 0))],
        out_specs=pl.BlockSpec((None, rows, HEAD_DIM), lambda b, j, pt: (b, 0, 0)),
        scratch_shapes=[pltpu.VMEM((rows, HEAD_DIM), F32),
                        pltpu.VMEM((rows, PAGE), F32),
                        pltpu.VMEM((PAGE * heads, HEAD_DIM), F32),
                        pltpu.VMEM((PAGE * heads, HEAD_DIM), F32)],
    )
    return pl.pallas_call(
        functools.partial(_attn_sample_kernel, heads=heads, tq=tq, n_new=n_new),
        grid_spec=grid_spec,
        out_shape=jax.ShapeDtypeStruct((db, rows, HEAD_DIM), F32),
        compiler_params=_params(("parallel", "arbitrary")),
    )(page_table.reshape(-1), q_s, bias_rows, k_new, v_new,
      *([cache_k] * npg), *([cache_v] * npg), uo)


def _pool_prompt_kernel(u_ref, halo_ref, w_ref, sc_ref, o_ref, ext_ref, *, front, tr, gdim):
    i = pl.program_id(0)
    halo = halo_ref[...]
    ext_ref[0:POOL_HALO, :] = jnp.where(i > 0, halo, 0.0)
    ext_ref[POOL_HALO:, :] = u_ref[...]
    pos = i * tr + lax.broadcasted_iota(jnp.int32, (tr, gdim), 0) - front
    for g, w in enumerate(POOL_WINDOWS):
        cols = slice(g * gdim, (g + 1) * gdim)
        u = ext_ref[POOL_HALO:, cols]
        win = u
        for k in range(1, w):
            win = win + ext_ref[POOL_HALO - k:POOL_HALO - k + tr, cols]
        cnt = jnp.clip(pos + 1, 1, w).astype(F32)
        pooled = (win / cnt - u).astype(BF16)
        y = jnp.dot(pooled, w_ref[g].astype(BF16), preferred_element_type=F32)
        o_ref[:, cols] = (y * sc_ref[:, cols]).astype(o_ref.dtype)


def _pool_prompt(p, w_pool, pool_scale, *, rows, u_col0, front, tr):
    ng, gdim, _ = w_pool.shape
    width = ng * gdim
    cb = u_col0 // width
    hb = tr // POOL_HALO
    return pl.pallas_call(
        functools.partial(_pool_prompt_kernel, front=front, tr=tr, gdim=gdim),
        grid=(rows // tr,),
        in_specs=[pl.BlockSpec((tr, width), lambda i: (i, cb)),
                  pl.BlockSpec((POOL_HALO, width), lambda i: (jnp.maximum(i * hb - 1, 0), cb)),
                  pl.BlockSpec((ng, gdim, gdim), lambda i: (0, 0, 0)),
                  pl.BlockSpec((1, width), lambda i: (0, 0))],
        out_specs=pl.BlockSpec((tr, width), lambda i: (i, 0)),
        out_shape=jax.ShapeDtypeStruct((rows, width), BF16),
        scratch_shapes=[pltpu.VMEM((POOL_HALO + tr, width), F32)],
        compiler_params=_params(("parallel",)),
    )(p, p, w_pool, pool_scale.reshape(1, width))


def _pool_sample_kernel(st_ref, u_ref, w_ref, sc_ref, o_ref, *, gdim, n_new):
    ctx = st_ref.shape[0]
    rows = [st_ref[r] for r in range(ctx)] + [u_ref[r] for r in range(n_new)]
    for g, w in enumerate(POOL_WINDOWS):
        cols = slice(g * gdim, (g + 1) * gdim)
        wg = w_ref[g].astype(BF16)
        for s in range(n_new):
            last = ctx + s
            win = rows[last][:, cols]
            for k in range(1, w):
                win = win + rows[last - k][:, cols]
            pooled = (win / float(w) - rows[last][:, cols]).astype(BF16)
            y = jnp.dot(pooled, wg, preferred_element_type=F32)
            o_ref[s, :, cols] = (y * sc_ref[:, cols]).astype(o_ref.dtype)


def _pool_sample(state_t, u_t, w_pool, pool_scale, *, tb):
    ctx, db, width = state_t.shape
    n_new = u_t.shape[0]
    ng, gdim, _ = w_pool.shape
    assert ctx >= max(POOL_WINDOWS) - 1
    return pl.pallas_call(
        functools.partial(_pool_sample_kernel, gdim=gdim, n_new=n_new),
        grid=(db // tb,),
        in_specs=[pl.BlockSpec((ctx, tb, width), lambda i: (0, i, 0)),
                  pl.BlockSpec((n_new, tb, width), lambda i: (0, i, 0)),
                  pl.BlockSpec((ng, gdim, gdim), lambda i: (0, 0, 0)),
                  pl.BlockSpec((1, width), lambda i: (0, 0))],
        out_specs=pl.BlockSpec((n_new, tb, width), lambda i: (0, i, 0)),
        out_shape=jax.ShapeDtypeStruct((n_new, db, width), BF16),
        compiler_params=_params(("parallel",)),
    )(state_t, u_t, w_pool, pool_scale.reshape(1, width))


def kernel(x_prompt, x_sample, cache_k, cache_v, state_pool, page_table, meta_tokens, g_pre_mix, w_in, gate_bias, sb_bias, w_pool, pool_scale, w_up_attn, w_up_pool, w_out, g_post_mix, g_pre_mlp, w_ff1, w_ff2, g_post_mlp):
    batch, seq, d = x_prompt.shape
    db, ds, _ = x_sample.shape
    depth = w_in.shape[0]
    n_meta = meta_tokens.shape[0]
    n_phys, heads = cache_k.shape[1], cache_k.shape[3]
    sbw = heads * HEAD_DIM
    ng, gdim = w_pool.shape[1], w_pool.shape[2]
    pw = ng * gdim
    assert batch == 1 and depth == 1, "one prompt sequence, one layer"
    assert cache_k.shape[2] == PAGE and cache_k.shape[4] == HEAD_DIM
    assert seq % ABLK == 0 and n_meta <= ABLK and ds <= 8
    assert heads % 8 == 0 and heads % HEAD_GROUP == 0
    assert state_pool.shape[2] == POOL_CTX and sbw == pw

    front = ABLK - n_meta
    rows_p = ABLK + seq
    n_s = db * ds
    rows = rows_p + n_s
    q0, k0, v0, u0, g0 = 0, sbw, 2 * sbw, 3 * sbw, 3 * sbw + pw

    x_all = jnp.concatenate([jnp.zeros((front, d), F32), meta_tokens.astype(F32),
                             x_prompt[0], x_sample.reshape(n_s, d)], axis=0)

    tm = _pick(rows, 1200, 16)
    te = _pick(rows, 400, 16)
    tn = 512

    h0 = _norm_cast(x_all, g_pre_mix[0], te)
    p, qkv = _in_proj(h0, w_in[0], gate_bias[0], tm=tm, tn=tn, q_cols=k0, qkv_cols=u0)

    o_attn_p = _attn_prompt(qkv, sb_bias[0], rows=rows_p, heads=heads)

    tq = 8
    p_s = p[rows_p:]
    q_s = qkv[rows_p:, q0:q0 + sbw].astype(F32).reshape(db, ds, heads, HEAD_DIM)
    q_s = jnp.pad(q_s.transpose(0, 2, 1, 3), ((0, 0), (0, 0), (0, tq - ds), (0, 0)))
    q_s = q_s.reshape(db, heads * tq, HEAD_DIM)
    k_s = p_s[:, k0:k0 + sbw]
    v_s = p_s[:, v0:v0 + sbw]
    bias_rows = jnp.broadcast_to(jnp.repeat(sb_bias[0].astype(F32) * LOG2E, tq)[:, None],
                                 (heads * tq, PAGE))
    o_s = _attn_sample(q_s, bias_rows, k_s.reshape(db, ds * heads, HEAD_DIM),
                       v_s.reshape(db, ds * heads, HEAD_DIM),
                       cache_k.reshape(n_phys, PAGE * heads, HEAD_DIM),
                       cache_v.reshape(n_phys, PAGE * heads, HEAD_DIM),
                       page_table, heads=heads, tq=tq)
    o_attn_s = o_s.reshape(db, heads, tq, HEAD_DIM)[:, :, :ds].transpose(0, 2, 1, 3)
    o_attn = jnp.concatenate([o_attn_p, o_attn_s.reshape(n_s, sbw).astype(BF16)], axis=0)

    o_pool_p = _pool_prompt(p, w_pool[0], pool_scale[0], rows=rows_p, u_col0=u0, front=front,
                            tr=_pick(rows_p, 700, 16))
    u_s = p_s[:, u0:u0 + pw].reshape(db, ds, pw)
    o_pool_s = _pool_sample(state_pool[0].transpose(1, 0, 2), u_s.transpose(1, 0, 2),
                            w_pool[0], pool_scale[0], tb=_pick(db, 32, 8))
    o_pool = jnp.concatenate([o_pool_p, o_pool_s.transpose(1, 0, 2).reshape(n_s, pw)], axis=0)

    m = _up_gate(o_attn, o_pool, w_up_attn[0], w_up_pool[0], p, g0, tm=tm, tn=tn // 2)
    mix = _matmul(m, w_out[0], tm=tm, tn=tn, tk=d, out_dtype=F32)
    x1, h2 = _resid_norm2(x_all, mix, g_post_mix[0], g_pre_mlp[0], te)

    dff = w_ff1.shape[2]
    hid = _matmul(h2, w_ff1[0], tm=tm, tn=tn, tk=d, out_dtype=BF16, epilogue="relu2")
    ffo = _matmul(hid, w_ff2[0], tm=tm, tn=2 * tn, tk=_pick(dff, 2048, 128), out_dtype=F32)
    y = _resid_norm(x1, ffo, g_post_mlp[0], te)

    y_prompt = y[ABLK:rows_p][None]
    y_sample = y[rows_p:].reshape(db, ds, d)
    kv_shape = (1, 1, n_meta + seq, heads, HEAD_DIM)
    k_prompt = p[front:rows_p, k0:k0 + sbw].reshape(kv_shape)
    v_prompt = p[front:rows_p, v0:v0 + sbw].reshape(kv_shape)
    pool_prompt = p[rows_p - POOL_CTX:rows_p, u0:u0 + pw][None, None]
    k_sample = k_s.reshape(1, db, ds, heads, HEAD_DIM)
    v_sample = v_s.reshape(1, db, ds, heads, HEAD_DIM)
    pool_sample = jnp.concatenate([state_pool[0, :, ds:], u_s], axis=1)[None]
    return (y_prompt, y_sample, k_prompt, v_prompt, pool_prompt, k_sample, v_sample, pool_sample)
```

```python
import functools
import math

import jax
import jax.numpy as jnp
from jax import lax
from jax.experimental import pallas as pl
from jax.experimental.pallas import tpu as pltpu

HEAD_DIM = 128
PAGE = 128
ABLK = 256
HEAD_GROUP = 4
PAGES_PER_STEP = 8
POOL_WINDOWS = (2, 4, 8, 16)
POOL_CTX = max(POOL_WINDOWS) - 1
POOL_HALO = 16
NORM_EPS = 1e-6
LOG2E = 1.4426950408889634
SB_SCALE2 = LOG2E / math.sqrt(HEAD_DIM)
MASKED = -1e30
VMEM_LIMIT = 56 * 1024 * 1024

F32 = jnp.float32
BF16 = jnp.bfloat16
NT_DIMS = (((1,), (1,)), ((), ()))


def _params(sem, vmem=VMEM_LIMIT):
    return pltpu.CompilerParams(dimension_semantics=sem, vmem_limit_bytes=vmem)


def _pick(n, cap, align):
    best = None
    for d in range(align, min(n, cap) + 1, align):
        if n % d == 0:
            best = d
    assert best is not None, (n, cap, align)
    return best


def _log2_fail(z):
    nz = -z
    e = jnp.exp2(jnp.minimum(z, nz))
    return jnp.minimum(nz, 0.0) - jnp.log(1.0 + e) * LOG2E


def _rms(x, g):
    return x * lax.rsqrt(jnp.mean(x * x, axis=-1, keepdims=True) + NORM_EPS) * g


def _tile_maps(n_p, n_sb):
    prompt = lambda i: (jnp.clip(i - 1, 0, n_p - 1), 0)
    sample = lambda i: (jnp.clip(i - 1 - n_p, 0, n_sb - 1), 0)
    return prompt, sample


def _x_specs(d, n_p, n_sb):
    prompt, sample = _tile_maps(n_p, n_sb)
    return [pl.BlockSpec((ABLK, d), lambda i: (0, 0)),
            pl.BlockSpec((ABLK, d), prompt),
            pl.BlockSpec((ABLK, d), sample)]


def _for_x_tile(n_p, head_ref, xp_ref, xs_ref, fn):
    i = pl.program_id(0)

    @pl.when(i == 0)
    def _():
        fn(head_ref[...])

    @pl.when((i >= 1) & (i <= n_p))
    def _():
        fn(xp_ref[...])

    @pl.when(i > n_p)
    def _():
        fn(xs_ref[...])


def _norm_cast_kernel(head_ref, xp_ref, xs_ref, g_ref, h_ref, *, n_p):
    def fn(x):
        h_ref[...] = _rms(x, g_ref[...]).astype(h_ref.dtype)
    _for_x_tile(n_p, head_ref, xp_ref, xs_ref, fn)


def _norm_cast(x_head, x_prompt, x_sample, g):
    d = x_head.shape[1]
    n_p, n_sb = x_prompt.shape[0] // ABLK, x_sample.shape[0] // ABLK
    n_t = 1 + n_p + n_sb
    return pl.pallas_call(
        functools.partial(_norm_cast_kernel, n_p=n_p),
        grid=(n_t,),
        in_specs=_x_specs(d, n_p, n_sb) + [pl.BlockSpec((1, d), lambda i: (0, 0))],
        out_specs=pl.BlockSpec((ABLK, d), lambda i: (i, 0)),
        out_shape=jax.ShapeDtypeStruct((n_t * ABLK, d), BF16),
        compiler_params=_params(("parallel",)),
    )(x_head, x_prompt, x_sample, g.reshape(1, d))


def _resid_norm2_kernel(head_ref, xp_ref, xs_ref, y_ref, g_ref, g2_ref, o_ref, h_ref, *, n_p):
    def fn(x):
        x1 = x + _rms(y_ref[...], g_ref[...])
        o_ref[...] = x1
        h_ref[...] = _rms(x1, g2_ref[...]).astype(h_ref.dtype)
    _for_x_tile(n_p, head_ref, xp_ref, xs_ref, fn)


def _resid_norm2(x_head, x_prompt, x_sample, y, g, g2):
    r, d = y.shape
    n_p, n_sb = x_prompt.shape[0] // ABLK, x_sample.shape[0] // ABLK
    row = pl.BlockSpec((ABLK, d), lambda i: (i, 0))
    vec = pl.BlockSpec((1, d), lambda i: (0, 0))
    return pl.pallas_call(
        functools.partial(_resid_norm2_kernel, n_p=n_p),
        grid=(r // ABLK,),
        in_specs=_x_specs(d, n_p, n_sb) + [row, vec, vec],
        out_specs=[row, row],
        out_shape=[jax.ShapeDtypeStruct((r, d), F32), jax.ShapeDtypeStruct((r, d), BF16)],
        compiler_params=_params(("parallel",)),
    )(x_head, x_prompt, x_sample, y, g.reshape(1, d), g2.reshape(1, d))


def _resid_norm_out_kernel(x_ref, y_ref, g_ref, yp_ref, ys_ref, *, n_p):
    i = pl.program_id(0)

    @pl.when((i >= 1) & (i <= n_p))
    def _():
        yp_ref[...] = x_ref[...] + _rms(y_ref[...], g_ref[...])

    @pl.when(i > n_p)
    def _():
        ys_ref[...] = x_ref[...] + _rms(y_ref[...], g_ref[...])


def _resid_norm_out(x, y, g, *, n_p, n_sb):
    r, d = x.shape
    prompt, sample = _tile_maps(n_p, n_sb)
    row = pl.BlockSpec((ABLK, d), lambda i: (i, 0))
    return pl.pallas_call(
        functools.partial(_resid_norm_out_kernel, n_p=n_p),
        grid=(r // ABLK,),
        in_specs=[row, row, pl.BlockSpec((1, d), lambda i: (0, 0))],
        out_specs=[pl.BlockSpec((ABLK, d), prompt), pl.BlockSpec((ABLK, d), sample)],
        out_shape=[jax.ShapeDtypeStruct((n_p * ABLK, d), F32),
                   jax.ShapeDtypeStruct((n_sb * ABLK, d), F32)],
        compiler_params=_params(("arbitrary",)),
    )(x, y, g.reshape(1, d))


def _column_halves(width):
    half = width // 2
    return [slice(0, half), slice(half, width)]


def _mm_kernel(a_ref, w_ref, *rest, epilogue, nk, out_scales):
    b_ref = rest[0] if epilogue == "sigmoid" else None
    o_refs = rest[1:] if epilogue == "sigmoid" else rest
    if nk > 1:
        @pl.when(pl.program_id(2) == 0)
        def _():
            o_refs[0][...] = jnp.zeros_like(o_refs[0])

    for cols in _column_halves(o_refs[0].shape[1]):
        part = jnp.dot(a_ref[...], w_ref[:, cols].astype(BF16), preferred_element_type=F32)
        if nk > 1:
            o_refs[0][:, cols] += part
            continue
        if epilogue == "relu2":
            part = jnp.square(jnp.maximum(part, 0.0))
        elif epilogue == "sigmoid":
            part = 1.0 / (1.0 + jnp.exp(-(part + b_ref[:, cols])))
        for o_ref, scale in zip(o_refs, out_scales):
            o_ref[:, cols] = (part if scale is None else part * scale).astype(o_ref.dtype)


def _matmul(a, w, *, tm, tn, tk, outs, col0=0, ncols=None, epilogue=None, bias=None):
    r, kdim = a.shape
    n = w.shape[1] - col0 if ncols is None else ncols
    nk = kdim // tk
    jb = col0 // tn
    assert col0 % tn == 0 and n % tn == 0
    assert nk == 1 or (outs == [(F32, None)] and epilogue is None)
    in_specs = [pl.BlockSpec((tm, tk), lambda i, j, k: (i, k)),
                pl.BlockSpec((tk, tn), lambda i, j, k: (k, jb + j))]
    args = [a, w]
    if epilogue == "sigmoid":
        in_specs.append(pl.BlockSpec((1, tn), lambda i, j, k: (0, j)))
        args.append(bias.reshape(1, n))
    res = pl.pallas_call(
        functools.partial(_mm_kernel, epilogue=epilogue, nk=nk,
                          out_scales=[s for _, s in outs]),
        grid=(r // tm, n // tn, nk),
        in_specs=in_specs,
        out_specs=[pl.BlockSpec((tm, tn), lambda i, j, k: (i, j)) for _ in outs],
        out_shape=[jax.ShapeDtypeStruct((r, n), dt) for dt, _ in outs],
        compiler_params=_params(("parallel", "parallel", "arbitrary")),
    )(*args)
    return res[0] if len(outs) == 1 else res


def _up_gate_kernel(oa_ref, op_ref, wa_ref, wp_ref, ga_ref, gp_ref, o_ref):
    ya = jnp.dot(oa_ref[...], wa_ref[...].astype(BF16), preferred_element_type=F32)
    yp = jnp.dot(op_ref[...], wp_ref[...].astype(BF16), preferred_element_type=F32)
    o_ref[...] = (ga_ref[...] * ya + gp_ref[...] * yp).astype(o_ref.dtype)


def _up_gate(o_attn, o_pool, w_up_attn, w_up_pool, gates, *, tm, tn):
    r, c = o_attn.shape
    d = w_up_attn.shape[1]
    ga0 = 0
    gp0 = d // tn
    return pl.pallas_call(
        _up_gate_kernel,
        grid=(r // tm, d // tn),
        in_specs=[pl.BlockSpec((tm, c), lambda i, j: (i, 0)),
                  pl.BlockSpec((tm, c), lambda i, j: (i, 0)),
                  pl.BlockSpec((c, tn), lambda i, j: (0, j)),
                  pl.BlockSpec((c, tn), lambda i, j: (0, j)),
                  pl.BlockSpec((tm, tn), lambda i, j: (i, ga0 + j)),
                  pl.BlockSpec((tm, tn), lambda i, j: (i, gp0 + j))],
        out_specs=pl.BlockSpec((tm, tn), lambda i, j: (i, j)),
        out_shape=jax.ShapeDtypeStruct((r, d), BF16),
        compiler_params=_params(("parallel", "parallel")),
    )(o_attn, o_pool, w_up_attn, w_up_pool, gates, gates)


def _attn_prompt_kernel(bias_ref, q_ref, k_ref, v_ref, tri_ref, dmask_ref, o_ref,
                        vt_ref, z_ref, a_ref, acc_ref, *, nblk):
    hg = pl.program_id(0)
    i = pl.program_id(1)
    lanes = [slice(g * HEAD_DIM, (g + 1) * HEAD_DIM) for g in range(HEAD_GROUP)]

    @pl.when(i == 0)
    def _():
        def transpose_block(c, carry):
            st = pl.multiple_of(c * ABLK, ABLK)
            for g in range(HEAD_GROUP):
                vt_ref[g, :, pl.ds(st, ABLK)] = (
                    v_ref[pl.ds(st, ABLK), lanes[g]].astype(F32).T.astype(BF16))
            return carry
        lax.fori_loop(0, nblk, transpose_block, 0)

    heads = range(HEAD_GROUP)
    bias2 = [bias_ref[hg * HEAD_GROUP + g] * LOG2E for g in heads]

    def scores(kb):
        st = pl.multiple_of(kb * ABLK, ABLK)
        return tuple(lax.dot_general(k_ref[pl.ds(st, ABLK), lanes[g]], q_ref[:, lanes[g]],
                                     NT_DIMS, preferred_element_type=F32) + bias2[g]
                     for g in heads)

    def weights(z, cb):
        lf = [_log2_fail(zg) for zg in z]
        later = [jnp.dot(tri_ref[...], lfg.astype(BF16), preferred_element_type=F32)
                 for lfg in lf]
        a = tuple(jnp.exp2(z[g] + lf[g] + (cb[g] + later[g])).astype(BF16) for g in heads)
        cb = tuple(cb[g] + (later[g][0:1, :] + lf[g][0:1, :]) for g in heads)
        return a, cb

    def attend(kb):
        st = pl.multiple_of(kb * ABLK, ABLK)
        for g in heads:
            acc_ref[g] += jnp.dot(vt_ref[g, :, pl.ds(st, ABLK)], a_ref[g],
                                  preferred_element_type=F32)

    def stash(z, a):
        for g in heads:
            z_ref[g] = z[g]
            a_ref[g] = a[g]

    acc_ref[...] = jnp.zeros_like(acc_ref)
    cb = (jnp.zeros((1, ABLK), F32),) * HEAD_GROUP
    a, cb = weights([zg + dmask_ref[...] for zg in scores(i)], cb)
    stash(scores(jnp.maximum(i - 1, 0)), a)

    def body(t, cb):
        kb = i - 1 - t
        attend(kb + 1)
        z_next = scores(jnp.maximum(kb - 1, 0))
        a, cb = weights([z_ref[g] for g in heads], cb)
        stash(z_next, a)
        return cb

    lax.fori_loop(0, i, body, cb)
    attend(0)
    for g in heads:
        o_ref[:, lanes[g]] = acc_ref[g].T.astype(o_ref.dtype)


def _attn_prompt(q16, kv16, sb_bias, *, rows, heads):
    nblk = rows // ABLK
    ngrp = heads // HEAD_GROUP
    gw = HEAD_GROUP * HEAD_DIM
    r = lax.broadcasted_iota(jnp.int32, (ABLK, ABLK), 0)
    c = lax.broadcasted_iota(jnp.int32, (ABLK, ABLK), 1)
    tri = jnp.where(c > r, 1.0, 0.0).astype(BF16)
    dmask = jnp.where(r < c, 0.0, MASKED).astype(F32)
    return pl.pallas_call(
        functools.partial(_attn_prompt_kernel, nblk=nblk),
        grid=(ngrp, nblk),
        in_specs=[pl.BlockSpec(memory_space=pltpu.SMEM),
                  pl.BlockSpec((ABLK, gw), lambda h, i: (i, h)),
                  pl.BlockSpec((rows, gw), lambda h, i: (0, h)),
                  pl.BlockSpec((rows, gw), lambda h, i: (0, ngrp + h)),
                  pl.BlockSpec((ABLK, ABLK), lambda h, i: (0, 0)),
                  pl.BlockSpec((ABLK, ABLK), lambda h, i: (0, 0))],
        out_specs=pl.BlockSpec((ABLK, gw), lambda h, i: (i, h)),
        out_shape=jax.ShapeDtypeStruct((rows, heads * HEAD_DIM), BF16),
        scratch_shapes=[pltpu.VMEM((HEAD_GROUP, HEAD_DIM, rows), BF16),
                        pltpu.VMEM((HEAD_GROUP, ABLK, ABLK), F32),
                        pltpu.VMEM((HEAD_GROUP, ABLK, ABLK), BF16),
                        pltpu.VMEM((HEAD_GROUP, HEAD_DIM, ABLK), F32)],
        compiler_params=_params(("parallel", "arbitrary")),
    )(sb_bias, q16, kv16, kv16, tri, dmask)


def _attn_sample_kernel(pt_ref, q_ref, bias_ref, kn_ref, vn_ref, *rest, heads, tq, n_new):
    npg = PAGES_PER_STEP
    kp_refs, vp_refs = rest[:npg], rest[npg:2 * npg]
    uo_ref, o_ref, acc_ref, cb_ref, kbuf_ref, vbuf_ref = rest[2 * npg:]
    del pt_ref
    j = pl.program_id(1)
    uo = uo_ref[...]
    bias2 = bias_ref[...]

    def by_head(ref):
        x = ref[...].reshape(PAGE, heads, HEAD_DIM)
        return pltpu.einshape("khd->hkd", x)

    def block(k_ref, v_ref, cb, outs, mask):
        kh = by_head(k_ref)
        vh = by_head(v_ref)
        z = jnp.concatenate(
            [lax.dot_general(q_ref[hh * tq:(hh + 1) * tq, :].astype(BF16),
                             kh[hh].astype(BF16),
                             NT_DIMS, preferred_element_type=F32)
             for hh in range(heads)], axis=0) + bias2
        if mask is not None:
            z = z + mask
        lf = _log2_fail(z)
        r = jnp.dot(lf.astype(BF16), uo, preferred_element_type=F32)
        a = jnp.exp2(z + lf + (cb + r[:, :PAGE]))
        outs = [o + jnp.dot(a[hh * tq:(hh + 1) * tq, :].astype(BF16),
                            vh[hh].astype(BF16),
                            preferred_element_type=F32)
                for hh, o in enumerate(outs)]
        return cb + r[:, PAGE:], outs

    @pl.when(j == 0)
    def _():
        kbuf_ref[...] = jnp.zeros_like(kbuf_ref)
        vbuf_ref[...] = jnp.zeros_like(vbuf_ref)
        kbuf_ref[0:n_new * heads, :] = kn_ref[...]
        vbuf_ref[0:n_new * heads, :] = vn_ref[...]
        row = lax.broadcasted_iota(jnp.int32, (heads * tq, PAGE), 0)
        col = lax.broadcasted_iota(jnp.int32, (heads * tq, PAGE), 1)
        mask = jnp.where(col < (row & (tq - 1)), 0.0, MASKED)
        cb, outs = block(kbuf_ref, vbuf_ref, jnp.zeros((heads * tq, PAGE), F32),
                         [jnp.zeros((tq, HEAD_DIM), F32)] * heads, mask)
        cb_ref[...] = cb
        acc_ref[...] = jnp.concatenate(outs, axis=0)

    cb = cb_ref[...]
    outs = [jnp.zeros((tq, HEAD_DIM), F32)] * heads
    for c in range(npg):
        cb, outs = block(kp_refs[c], vp_refs[c], cb, outs, None)
    cb_ref[...] = cb
    acc_ref[...] += jnp.concatenate(outs, axis=0)

    @pl.when(j == pl.num_programs(1) - 1)
    def _():
        o_ref[...] = acc_ref[...]


def _attn_sample(q_s, bias_rows, k_new, v_new, cache_k, cache_v, page_table, *, heads, tq):
    db, n_pages = page_table.shape
    rows = heads * tq
    npg = PAGES_PER_STEP
    assert n_pages % npg == 0 and tq & (tq - 1) == 0
    n_new = k_new.shape[1] // heads

    def page_map(c):
        return lambda b, j, pt: (pt[b * n_pages + (n_pages - 1 - (j * npg + c))], 0, 0)

    page_specs = [pl.BlockSpec((None, PAGE * heads, HEAD_DIM), page_map(c)) for c in range(npg)]
    uo_r = lax.broadcasted_iota(jnp.int32, (PAGE, 2 * PAGE), 0)
    uo_c = lax.broadcasted_iota(jnp.int32, (PAGE, 2 * PAGE), 1)
    uo = jnp.where((uo_r > uo_c) | (uo_c >= PAGE), 1.0, 0.0).astype(BF16)
    grid_spec = pltpu.PrefetchScalarGridSpec(
        num_scalar_prefetch=1,
        grid=(db, n_pages // npg),
        in_specs=[pl.BlockSpec((None, rows, HEAD_DIM), lambda b, j, pt: (b, 0, 0)),
                  pl.BlockSpec((rows, PAGE), lambda b, j, pt: (0, 0)),
                  pl.BlockSpec((None, n_new * heads, HEAD_DIM), lambda b, j, pt: (b, 0, 0)),
                  pl.BlockSpec((None, n_new * heads, HEAD_DIM), lambda b, j, pt: (b, 0, 0))]
                 + page_specs + page_specs
                 + [pl.BlockSpec((PAGE, 2 * PAGE), lambda b, j, pt: (0, 0))],
        out_specs=pl.BlockSpec((None, rows, HEAD_DIM), lambda b, j, pt: (b, 0, 0)),
        scratch_shapes=[pltpu.VMEM((rows, HEAD_DIM), F32),
                        pltpu.VMEM((rows, PAGE), F32),
                        pltpu.VMEM((PAGE * heads, HEAD_DIM), F32),
                        pltpu.VMEM((PAGE * heads, HEAD_DIM), F32)],
    )
    return pl.pallas_call(
        functools.partial(_attn_sample_kernel, heads=heads, tq=tq, n_new=n_new),
        grid_spec=grid_spec,
        out_shape=jax.ShapeDtypeStruct((db, rows, HEAD_DIM), F32),
        compiler_params=_params(("parallel", "arbitrary")),
    )(page_table.reshape(-1), q_s, bias_rows, k_new, v_new,
      *([cache_k] * npg), *([cache_v] * npg), uo)


def _pool_prompt_kernel(u_ref, halo_ref, w_ref, sc_ref, o_ref, ext_ref, *, front, tr, gdim):
    i = pl.program_id(0)
    halo = halo_ref[...]
    ext_ref[0:POOL_HALO, :] = jnp.where(i > 0, halo, 0.0)
    ext_ref[POOL_HALO:, :] = u_ref[...]
    pos = i * tr + lax.broadcasted_iota(jnp.int32, (tr, gdim), 0) - front
    for g, w in enumerate(POOL_WINDOWS):
        cols = slice(g * gdim, (g + 1) * gdim)
        u = ext_ref[POOL_HALO:, cols]
        win = u
        for k in range(1, w):
            win = win + ext_ref[POOL_HALO - k:POOL_HALO - k + tr, cols]
        cnt = jnp.clip(pos + 1, 1, w).astype(F32)
        pooled = (win / cnt - u).astype(BF16)
        y = jnp.dot(pooled, w_ref[g].astype(BF16), preferred_element_type=F32)
        o_ref[:, cols] = (y * sc_ref[:, cols]).astype(o_ref.dtype)


def _pool_prompt(u, w_pool, pool_scale, *, rows, front, tr):
    ng, gdim, _ = w_pool.shape
    width = ng * gdim
    cb = 0
    hb = tr // POOL_HALO
    return pl.pallas_call(
        functools.partial(_pool_prompt_kernel, front=front, tr=tr, gdim=gdim),
        grid=(rows // tr,),
        in_specs=[pl.BlockSpec((tr, width), lambda i: (i, cb)),
                  pl.BlockSpec((POOL_HALO, width), lambda i: (jnp.maximum(i * hb - 1, 0), cb)),
                  pl.BlockSpec((ng, gdim, gdim), lambda i: (0, 0, 0)),
                  pl.BlockSpec((1, width), lambda i: (0, 0))],
        out_specs=pl.BlockSpec((tr, width), lambda i: (i, 0)),
        out_shape=jax.ShapeDtypeStruct((rows, width), BF16),
        scratch_shapes=[pltpu.VMEM((POOL_HALO + tr, width), F32)],
        compiler_params=_params(("parallel",)),
    )(u, u, w_pool, pool_scale.reshape(1, width))


def _pool_sample_kernel(st_ref, u_ref, w_ref, sc_ref, o_ref, *, gdim, n_new):
    ctx = st_ref.shape[0]
    rows = [st_ref[r] for r in range(ctx)] + [u_ref[r] for r in range(n_new)]
    for g, w in enumerate(POOL_WINDOWS):
        cols = slice(g * gdim, (g + 1) * gdim)
        wg = w_ref[g].astype(BF16)
        for s in range(n_new):
            last = ctx + s
            win = rows[last][:, cols]
            for k in range(1, w):
                win = win + rows[last - k][:, cols]
            pooled = (win / float(w) - rows[last][:, cols]).astype(BF16)
            y = jnp.dot(pooled, wg, preferred_element_type=F32)
            o_ref[s, :, cols] = (y * sc_ref[:, cols]).astype(o_ref.dtype)


def _pool_sample(state_t, u_t, w_pool, pool_scale, *, tb):
    ctx, db, width = state_t.shape
    n_new = u_t.shape[0]
    ng, gdim, _ = w_pool.shape
    assert ctx >= max(POOL_WINDOWS) - 1
    return pl.pallas_call(
        functools.partial(_pool_sample_kernel, gdim=gdim, n_new=n_new),
        grid=(db // tb,),
        in_specs=[pl.BlockSpec((ctx, tb, width), lambda i: (0, i, 0)),
                  pl.BlockSpec((n_new, tb, width), lambda i: (0, i, 0)),
                  pl.BlockSpec((ng, gdim, gdim), lambda i: (0, 0, 0)),
                  pl.BlockSpec((1, width), lambda i: (0, 0))],
        out_specs=pl.BlockSpec((n_new, tb, width), lambda i: (0, i, 0)),
        out_shape=jax.ShapeDtypeStruct((n_new, db, width), BF16),
        compiler_params=_params(("parallel",)),
    )(state_t, u_t, w_pool, pool_scale.reshape(1, width))


def kernel(x_prompt, x_sample, cache_k, cache_v, state_pool, page_table, meta_tokens, g_pre_mix, w_in, gate_bias, sb_bias, w_pool, pool_scale, w_up_attn, w_up_pool, w_out, g_post_mix, g_pre_mlp, w_ff1, w_ff2, g_post_mlp):
    batch, seq, d = x_prompt.shape
    db, ds, _ = x_sample.shape
    depth = w_in.shape[0]
    n_meta = meta_tokens.shape[0]
    n_phys, heads = cache_k.shape[1], cache_k.shape[3]
    sbw = heads * HEAD_DIM
    ng, gdim = w_pool.shape[1], w_pool.shape[2]
    pw = ng * gdim
    assert batch == 1 and depth == 1, "one prompt sequence, one layer"
    assert cache_k.shape[2] == PAGE and cache_k.shape[4] == HEAD_DIM
    assert seq % ABLK == 0 and n_meta <= ABLK and ds <= 8
    assert heads % 8 == 0 and heads % HEAD_GROUP == 0
    assert state_pool.shape[2] == POOL_CTX and sbw == pw

    front = ABLK - n_meta
    rows_p = ABLK + seq
    n_s = db * ds
    rows = rows_p + n_s
    k0, u0, g0 = sbw, 3 * sbw, 3 * sbw + pw
    assert n_s % ABLK == 0
    n_p, n_sb = seq // ABLK, n_s // ABLK

    x_head = jnp.concatenate([jnp.zeros((front, d), F32), meta_tokens.astype(F32)], axis=0)
    x_p, x_s = x_prompt[0], x_sample.reshape(n_s, d)

    tm = _pick(rows, 1200, 16)
    tn = 512
    dense = functools.partial(_matmul, tm=tm, tn=tn)

    h0 = _norm_cast(x_head, x_p, x_s, g_pre_mix[0])
    q16 = dense(h0, w_in[0], tk=d, col0=0, ncols=sbw, outs=[(BF16, SB_SCALE2)])
    kv, kv16 = dense(h0, w_in[0], tk=d, col0=k0, ncols=2 * sbw, outs=[(F32, None), (BF16, None)])
    u = dense(h0, w_in[0], tk=d, col0=u0, ncols=pw, outs=[(F32, None)])
    gates = dense(h0, w_in[0], tk=d, col0=g0, outs=[(F32, None)], epilogue="sigmoid",
                  bias=gate_bias[0])

    o_attn_p = _attn_prompt(q16, kv16, sb_bias[0], rows=rows_p, heads=heads)

    tq = 8
    q_s = q16[rows_p:].astype(F32).reshape(db, ds, heads, HEAD_DIM)
    q_s = jnp.pad(q_s.transpose(0, 2, 1, 3), ((0, 0), (0, 0), (0, tq - ds), (0, 0)))
    q_s = q_s.reshape(db, heads * tq, HEAD_DIM)
    k_s = kv[rows_p:, :sbw]
    v_s = kv[rows_p:, sbw:]
    bias_rows = jnp.broadcast_to(jnp.repeat(sb_bias[0].astype(F32) * LOG2E, tq)[:, None],
                                 (heads * tq, PAGE))
    o_s = _attn_sample(q_s, bias_rows, k_s.reshape(db, ds * heads, HEAD_DIM),
                       v_s.reshape(db, ds * heads, HEAD_DIM),
                       cache_k.reshape(n_phys, PAGE * heads, HEAD_DIM),
                       cache_v.reshape(n_phys, PAGE * heads, HEAD_DIM),
                       page_table, heads=heads, tq=tq)
    o_attn_s = o_s.reshape(db, heads, tq, HEAD_DIM)[:, :, :ds].transpose(0, 2, 1, 3)
    o_attn = jnp.concatenate([o_attn_p, o_attn_s.reshape(n_s, sbw).astype(BF16)], axis=0)

    o_pool_p = _pool_prompt(u, w_pool[0], pool_scale[0], rows=rows_p, front=front,
                            tr=_pick(rows_p, 700, 16))
    u_s = u[rows_p:].reshape(db, ds, pw)
    o_pool_s = _pool_sample(state_pool[0].transpose(1, 0, 2), u_s.transpose(1, 0, 2),
                            w_pool[0], pool_scale[0], tb=_pick(db, 32, 8))
    o_pool = jnp.concatenate([o_pool_p, o_pool_s.transpose(1, 0, 2).reshape(n_s, pw)], axis=0)

    m = _up_gate(o_attn, o_pool, w_up_attn[0], w_up_pool[0], gates, tm=tm, tn=tn // 2)
    mix = dense(m, w_out[0], tk=d, outs=[(F32, None)])
    x1, h2 = _resid_norm2(x_head, x_p, x_s, mix, g_post_mix[0], g_pre_mlp[0])

    dff = w_ff1.shape[2]
    hid = dense(h2, w_ff1[0], tk=d, outs=[(BF16, None)], epilogue="relu2")
    ffo = _matmul(hid, w_ff2[0], tm=tm, tn=2 * tn, tk=_pick(dff, 2048, 128), outs=[(F32, None)])
    y_p, y_s = _resid_norm_out(x1, ffo, g_post_mlp[0], n_p=n_p, n_sb=n_sb)

    y_prompt = y_p[None]
    y_sample = y_s.reshape(db, ds, d)
    kv_shape = (1, 1, n_meta + seq, heads, HEAD_DIM)
    k_prompt = kv[front:rows_p, :sbw].reshape(kv_shape)
    v_prompt = kv[front:rows_p, sbw:].reshape(kv_shape)
    pool_prompt = u[rows_p - POOL_CTX:rows_p][None, None]
    k_sample = k_s.reshape(1, db, ds, heads, HEAD_DIM)
    v_sample = v_s.reshape(1, db, ds, heads, HEAD_DIM)
    pool_sample = jnp.concatenate([state_pool[0, :, ds:], u_s], axis=1)[None]
    return (y_prompt, y_sample, k_prompt, v_prompt, pool_prompt, k_sample, v_sample, pool_sample)
```

```python
import functools
import math

import jax
import jax.numpy as jnp
from jax import lax
from jax.experimental import pallas as pl
from jax.experimental.pallas import tpu as pltpu

HEAD_DIM = 128
PAGE = 128
ABLK = 256
HEAD_GROUP = 4
PAGES_PER_STEP = 8
POOL_WINDOWS = (2, 4, 8, 16)
POOL_CTX = max(POOL_WINDOWS) - 1
POOL_HALO = 16
NORM_EPS = 1e-6
LOG2E = 1.4426950408889634
SB_SCALE2 = LOG2E / math.sqrt(HEAD_DIM)
MASKED = -1e30
VMEM_LIMIT = 56 * 1024 * 1024

F32 = jnp.float32
BF16 = jnp.bfloat16
NT_DIMS = (((1,), (1,)), ((), ()))


def _params(sem, vmem=VMEM_LIMIT):
    return pltpu.CompilerParams(dimension_semantics=sem, vmem_limit_bytes=vmem)


def _pick(n, cap, align):
    best = None
    for d in range(align, min(n, cap) + 1, align):
        if n % d == 0:
            best = d
    assert best is not None, (n, cap, align)
    return best


def _log2_fail(z):
    nz = -z
    e = jnp.exp2(jnp.minimum(z, nz))
    return jnp.minimum(nz, 0.0) - jnp.log(1.0 + e) * LOG2E


def _rms(x, g):
    return x * lax.rsqrt(jnp.mean(x * x, axis=-1, keepdims=True) + NORM_EPS) * g


def _tile_maps(n_p, n_sb):
    prompt = lambda i: (jnp.clip(i - 1, 0, n_p - 1), 0)
    sample = lambda i: (jnp.clip(i - 1 - n_p, 0, n_sb - 1), 0)
    return prompt, sample


def _x_specs(d, n_p, n_sb):
    prompt, sample = _tile_maps(n_p, n_sb)
    return [pl.BlockSpec((ABLK, d), lambda i: (0, 0)),
            pl.BlockSpec((ABLK, d), prompt),
            pl.BlockSpec((ABLK, d), sample)]


def _for_x_tile(n_p, head_ref, xp_ref, xs_ref, fn):
    i = pl.program_id(0)

    @pl.when(i == 0)
    def _():
        fn(head_ref[...])

    @pl.when((i >= 1) & (i <= n_p))
    def _():
        fn(xp_ref[...])

    @pl.when(i > n_p)
    def _():
        fn(xs_ref[...])


def _norm_cast_kernel(head_ref, xp_ref, xs_ref, g_ref, h_ref, *, n_p):
    def fn(x):
        h_ref[...] = _rms(x, g_ref[...]).astype(h_ref.dtype)
    _for_x_tile(n_p, head_ref, xp_ref, xs_ref, fn)


def _norm_cast(x_head, x_prompt, x_sample, g):
    d = x_head.shape[1]
    n_p, n_sb = x_prompt.shape[0] // ABLK, x_sample.shape[0] // ABLK
    n_t = 1 + n_p + n_sb
    return pl.pallas_call(
        functools.partial(_norm_cast_kernel, n_p=n_p),
        grid=(n_t,),
        in_specs=_x_specs(d, n_p, n_sb) + [pl.BlockSpec((1, d), lambda i: (0, 0))],
        out_specs=pl.BlockSpec((ABLK, d), lambda i: (i, 0)),
        out_shape=jax.ShapeDtypeStruct((n_t * ABLK, d), BF16),
        compiler_params=_params(("parallel",)),
    )(x_head, x_prompt, x_sample, g.reshape(1, d))


def _resid_norm2_kernel(head_ref, xp_ref, xs_ref, y_ref, g_ref, g2_ref, o_ref, h_ref, *, n_p):
    def fn(x):
        x1 = x + _rms(y_ref[...], g_ref[...])
        o_ref[...] = x1
        h_ref[...] = _rms(x1, g2_ref[...]).astype(h_ref.dtype)
    _for_x_tile(n_p, head_ref, xp_ref, xs_ref, fn)


def _resid_norm2(x_head, x_prompt, x_sample, y, g, g2):
    r, d = y.shape
    n_p, n_sb = x_prompt.shape[0] // ABLK, x_sample.shape[0] // ABLK
    row = pl.BlockSpec((ABLK, d), lambda i: (i, 0))
    vec = pl.BlockSpec((1, d), lambda i: (0, 0))
    return pl.pallas_call(
        functools.partial(_resid_norm2_kernel, n_p=n_p),
        grid=(r // ABLK,),
        in_specs=_x_specs(d, n_p, n_sb) + [row, vec, vec],
        out_specs=[row, row],
        out_shape=[jax.ShapeDtypeStruct((r, d), F32), jax.ShapeDtypeStruct((r, d), BF16)],
        compiler_params=_params(("parallel",)),
    )(x_head, x_prompt, x_sample, y, g.reshape(1, d), g2.reshape(1, d))


def _resid_norm_out_kernel(x_ref, y_ref, g_ref, yp_ref, ys_ref, *, n_p):
    i = pl.program_id(0)

    @pl.when((i >= 1) & (i <= n_p))
    def _():
        yp_ref[...] = x_ref[...] + _rms(y_ref[...], g_ref[...])

    @pl.when(i > n_p)
    def _():
        ys_ref[...] = x_ref[...] + _rms(y_ref[...], g_ref[...])


def _resid_norm_out(x, y, g, *, n_p, n_sb):
    r, d = x.shape
    prompt, sample = _tile_maps(n_p, n_sb)
    row = pl.BlockSpec((ABLK, d), lambda i: (i, 0))
    return pl.pallas_call(
        functools.partial(_resid_norm_out_kernel, n_p=n_p),
        grid=(r // ABLK,),
        in_specs=[row, row, pl.BlockSpec((1, d), lambda i: (0, 0))],
        out_specs=[pl.BlockSpec((ABLK, d), prompt), pl.BlockSpec((ABLK, d), sample)],
        out_shape=[jax.ShapeDtypeStruct((n_p * ABLK, d), F32),
                   jax.ShapeDtypeStruct((n_sb * ABLK, d), F32)],
        compiler_params=_params(("arbitrary",)),
    )(x, y, g.reshape(1, d))


def _column_halves(width):
    half = width // 2
    return [slice(0, half), slice(half, width)]


def _mm_kernel(a_ref, w_ref, *rest, epilogue, nk, out_scales):
    b_ref = rest[0] if epilogue == "sigmoid" else None
    o_refs = rest[1:] if epilogue == "sigmoid" else rest
    if nk > 1:
        @pl.when(pl.program_id(2) == 0)
        def _():
            o_refs[0][...] = jnp.zeros_like(o_refs[0])

    for cols in _column_halves(o_refs[0].shape[1]):
        part = jnp.dot(a_ref[...], w_ref[:, cols].astype(BF16), preferred_element_type=F32)
        if nk > 1:
            o_refs[0][:, cols] += part
            continue
        if epilogue == "relu2":
            part = jnp.square(jnp.maximum(part, 0.0))
        elif epilogue == "sigmoid":
            part = 1.0 / (1.0 + jnp.exp(-(part + b_ref[:, cols])))
        for o_ref, scale in zip(o_refs, out_scales):
            o_ref[:, cols] = (part if scale is None else part * scale).astype(o_ref.dtype)


def _matmul(a, w, *, tm, tn, tk, outs, col0=0, ncols=None, epilogue=None, bias=None):
    r, kdim = a.shape
    n = w.shape[1] - col0 if ncols is None else ncols
    nk = kdim // tk
    jb = col0 // tn
    assert col0 % tn == 0 and n % tn == 0
    assert nk == 1 or (outs == [(F32, None)] and epilogue is None)
    in_specs = [pl.BlockSpec((tm, tk), lambda i, j, k: (i, k)),
                pl.BlockSpec((tk, tn), lambda i, j, k: (k, jb + j))]
    args = [a, w]
    if epilogue == "sigmoid":
        in_specs.append(pl.BlockSpec((1, tn), lambda i, j, k: (0, j)))
        args.append(bias.reshape(1, n))
    res = pl.pallas_call(
        functools.partial(_mm_kernel, epilogue=epilogue, nk=nk,
                          out_scales=[s for _, s in outs]),
        grid=(r // tm, n // tn, nk),
        in_specs=in_specs,
        out_specs=[pl.BlockSpec((tm, tn), lambda i, j, k: (i, j)) for _ in outs],
        out_shape=[jax.ShapeDtypeStruct((r, n), dt) for dt, _ in outs],
        compiler_params=_params(("parallel", "parallel", "arbitrary")),
    )(*args)
    return res[0] if len(outs) == 1 else res


def _in_proj_kernel(a_ref, w_ref, b_ref, q16_ref, p_ref, kv16_ref, *, q_tiles, kv_tiles, gate_tile0):
    j = pl.program_id(1)
    halves = _column_halves(p_ref.shape[1])

    def product(cols):
        return jnp.dot(a_ref[...], w_ref[:, cols].astype(BF16), preferred_element_type=F32)

    @pl.when(j < q_tiles)
    def _():
        for cols in halves:
            q16_ref[:, cols] = (product(cols) * SB_SCALE2).astype(BF16)

    @pl.when((j >= q_tiles) & (j < q_tiles + kv_tiles))
    def _():
        for cols in halves:
            part = product(cols)
            p_ref[:, cols] = part
            kv16_ref[:, cols] = part.astype(BF16)

    @pl.when((j >= q_tiles + kv_tiles) & (j < gate_tile0))
    def _():
        for cols in halves:
            p_ref[:, cols] = product(cols)

    @pl.when(j >= gate_tile0)
    def _():
        for cols in halves:
            p_ref[:, cols] = 1.0 / (1.0 + jnp.exp(-(product(cols) + b_ref[:, cols])))


def _in_proj(h, w_in, gate_bias, *, tm, tn, q_cols, kv_cols):
    r, d = h.shape
    n = w_in.shape[1]
    n_gate = gate_bias.shape[0]
    q_tiles, kv_tiles, gate_tile0 = q_cols // tn, kv_cols // tn, (n - n_gate) // tn
    n_tiles = n // tn
    return pl.pallas_call(
        functools.partial(_in_proj_kernel, q_tiles=q_tiles, kv_tiles=kv_tiles,
                          gate_tile0=gate_tile0),
        grid=(r // tm, n_tiles),
        in_specs=[pl.BlockSpec((tm, d), lambda i, j: (i, 0)),
                  pl.BlockSpec((d, tn), lambda i, j: (0, j)),
                  pl.BlockSpec((1, tn), lambda i, j: (0, jnp.maximum(j - gate_tile0, 0)))],
        out_specs=[pl.BlockSpec((tm, tn), lambda i, j: (i, jnp.minimum(j, q_tiles - 1))),
                   pl.BlockSpec((tm, tn), lambda i, j: (i, jnp.maximum(j - q_tiles, 0))),
                   pl.BlockSpec((tm, tn),
                                lambda i, j: (i, jnp.clip(j - q_tiles, 0, kv_tiles - 1)))],
        out_shape=[jax.ShapeDtypeStruct((r, q_cols), BF16),
                   jax.ShapeDtypeStruct((r, n - q_cols), F32),
                   jax.ShapeDtypeStruct((r, kv_cols), BF16)],
        compiler_params=_params(("parallel", "arbitrary")),
    )(h, w_in, gate_bias.reshape(1, n_gate))


def _up_gate_kernel(oa_ref, op_ref, wa_ref, wp_ref, ga_ref, gp_ref, o_ref):
    ya = jnp.dot(oa_ref[...], wa_ref[...].astype(BF16), preferred_element_type=F32)
    yp = jnp.dot(op_ref[...], wp_ref[...].astype(BF16), preferred_element_type=F32)
    o_ref[...] = (ga_ref[...] * ya + gp_ref[...] * yp).astype(o_ref.dtype)


def _up_gate(o_attn, o_pool, w_up_attn, w_up_pool, gates, gate_col0, *, tm, tn):
    r, c = o_attn.shape
    d = w_up_attn.shape[1]
    ga0 = gate_col0 // tn
    gp0 = (gate_col0 + d) // tn
    return pl.pallas_call(
        _up_gate_kernel,
        grid=(r // tm, d // tn),
        in_specs=[pl.BlockSpec((tm, c), lambda i, j: (i, 0)),
                  pl.BlockSpec((tm, c), lambda i, j: (i, 0)),
                  pl.BlockSpec((c, tn), lambda i, j: (0, j)),
                  pl.BlockSpec((c, tn), lambda i, j: (0, j)),
                  pl.BlockSpec((tm, tn), lambda i, j: (i, ga0 + j)),
                  pl.BlockSpec((tm, tn), lambda i, j: (i, gp0 + j))],
        out_specs=pl.BlockSpec((tm, tn), lambda i, j: (i, j)),
        out_shape=jax.ShapeDtypeStruct((r, d), BF16),
        compiler_params=_params(("parallel", "parallel")),
    )(o_attn, o_pool, w_up_attn, w_up_pool, gates, gates)


def _attn_prompt_kernel(bias_ref, q_ref, k_ref, v_ref, tri_ref, dmask_ref, o_ref,
                        vt_ref, z_ref, a_ref, acc_ref, *, nblk):
    hg = pl.program_id(0)
    i = pl.program_id(1)
    lanes = [slice(g * HEAD_DIM, (g + 1) * HEAD_DIM) for g in range(HEAD_GROUP)]

    @pl.when(i == 0)
    def _():
        def transpose_block(c, carry):
            st = pl.multiple_of(c * ABLK, ABLK)
            for g in range(HEAD_GROUP):
                vt_ref[g, :, pl.ds(st, ABLK)] = (
                    v_ref[pl.ds(st, ABLK), lanes[g]].astype(F32).T.astype(BF16))
            return carry
        lax.fori_loop(0, nblk, transpose_block, 0)

    heads = range(HEAD_GROUP)
    bias2 = [bias_ref[hg * HEAD_GROUP + g] * LOG2E for g in heads]

    def scores(kb):
        st = pl.multiple_of(kb * ABLK, ABLK)
        return tuple(lax.dot_general(k_ref[pl.ds(st, ABLK), lanes[g]], q_ref[:, lanes[g]],
                                     NT_DIMS, preferred_element_type=F32) + bias2[g]
                     for g in heads)

    def weights(z, cb):
        lf = [_log2_fail(zg) for zg in z]
        later = [jnp.dot(tri_ref[...], lfg.astype(BF16), preferred_element_type=F32)
                 for lfg in lf]
        a = tuple(jnp.exp2(z[g] + lf[g] + (cb[g] + later[g])).astype(BF16) for g in heads)
        cb = tuple(cb[g] + (later[g][0:1, :] + lf[g][0:1, :]) for g in heads)
        return a, cb

    def attend(kb):
        st = pl.multiple_of(kb * ABLK, ABLK)
        for g in heads:
            acc_ref[g] += jnp.dot(vt_ref[g, :, pl.ds(st, ABLK)], a_ref[g],
                                  preferred_element_type=F32)

    def stash(z, a):
        for g in heads:
            z_ref[g] = z[g]
            a_ref[g] = a[g]

    acc_ref[...] = jnp.zeros_like(acc_ref)
    cb = (jnp.zeros((1, ABLK), F32),) * HEAD_GROUP
    a, cb = weights([zg + dmask_ref[...] for zg in scores(i)], cb)
    stash(scores(jnp.maximum(i - 1, 0)), a)

    def body(t, cb):
        kb = i - 1 - t
        attend(kb + 1)
        z_next = scores(jnp.maximum(kb - 1, 0))
        a, cb = weights([z_ref[g] for g in heads], cb)
        stash(z_next, a)
        return cb

    lax.fori_loop(0, i, body, cb)
    attend(0)
    for g in heads:
        o_ref[:, lanes[g]] = acc_ref[g].T.astype(o_ref.dtype)


def _attn_prompt(q16, kv16, sb_bias, *, rows, heads):
    nblk = rows // ABLK
    ngrp = heads // HEAD_GROUP
    gw = HEAD_GROUP * HEAD_DIM
    r = lax.broadcasted_iota(jnp.int32, (ABLK, ABLK), 0)
    c = lax.broadcasted_iota(jnp.int32, (ABLK, ABLK), 1)
    tri = jnp.where(c > r, 1.0, 0.0).astype(BF16)
    dmask = jnp.where(r < c, 0.0, MASKED).astype(F32)
    return pl.pallas_call(
        functools.partial(_attn_prompt_kernel, nblk=nblk),
        grid=(ngrp, nblk),
        in_specs=[pl.BlockSpec(memory_space=pltpu.SMEM),
                  pl.BlockSpec((ABLK, gw), lambda h, i: (i, h)),
                  pl.BlockSpec((rows, gw), lambda h, i: (0, h)),
                  pl.BlockSpec((rows, gw), lambda h, i: (0, ngrp + h)),
                  pl.BlockSpec((ABLK, ABLK), lambda h, i: (0, 0)),
                  pl.BlockSpec((ABLK, ABLK), lambda h, i: (0, 0))],
        out_specs=pl.BlockSpec((ABLK, gw), lambda h, i: (i, h)),
        out_shape=jax.ShapeDtypeStruct((rows, heads * HEAD_DIM), BF16),
        scratch_shapes=[pltpu.VMEM((HEAD_GROUP, HEAD_DIM, rows), BF16),
                        pltpu.VMEM((HEAD_GROUP, ABLK, ABLK), F32),
                        pltpu.VMEM((HEAD_GROUP, ABLK, ABLK), BF16),
                        pltpu.VMEM((HEAD_GROUP, HEAD_DIM, ABLK), F32)],
        compiler_params=_params(("parallel", "arbitrary")),
    )(sb_bias, q16, kv16, kv16, tri, dmask)


def _attn_sample_kernel(pt_ref, q_ref, bias_ref, kn_ref, vn_ref, *rest, heads, tq, n_new):
    npg = PAGES_PER_STEP
    kp_refs, vp_refs = rest[:npg], rest[npg:2 * npg]
    uo_ref, o_ref, acc_ref, cb_ref, kbuf_ref, vbuf_ref = rest[2 * npg:]
    del pt_ref
    j = pl.program_id(1)
    uo = uo_ref[...]
    bias2 = bias_ref[...]

    def by_head(ref):
        x = ref[...].reshape(PAGE, heads, HEAD_DIM)
        return pltpu.einshape("khd->hkd", x)

    def block(k_ref, v_ref, cb, outs, mask):
        kh = by_head(k_ref)
        vh = by_head(v_ref)
        z = jnp.concatenate(
            [lax.dot_general(q_ref[hh * tq:(hh + 1) * tq, :].astype(BF16),
                             kh[hh].astype(BF16),
                             NT_DIMS, preferred_element_type=F32)
             for hh in range(heads)], axis=0) + bias2
        if mask is not None:
            z = z + mask
        lf = _log2_fail(z)
        r = jnp.dot(lf.astype(BF16), uo, preferred_element_type=F32)
        a = jnp.exp2(z + lf + (cb + r[:, :PAGE]))
        outs = [o + jnp.dot(a[hh * tq:(hh + 1) * tq, :].astype(BF16),
                            vh[hh].astype(BF16),
                            preferred_element_type=F32)
                for hh, o in enumerate(outs)]
        return cb + r[:, PAGE:], outs

    @pl.when(j == 0)
    def _():
        kbuf_ref[...] = jnp.zeros_like(kbuf_ref)
        vbuf_ref[...] = jnp.zeros_like(vbuf_ref)
        kbuf_ref[0:n_new * heads, :] = kn_ref[...]
        vbuf_ref[0:n_new * heads, :] = vn_ref[...]
        row = lax.broadcasted_iota(jnp.int32, (heads * tq, PAGE), 0)
        col = lax.broadcasted_iota(jnp.int32, (heads * tq, PAGE), 1)
        mask = jnp.where(col < (row & (tq - 1)), 0.0, MASKED)
        cb, outs = block(kbuf_ref, vbuf_ref, jnp.zeros((heads * tq, PAGE), F32),
                         [jnp.zeros((tq, HEAD_DIM), F32)] * heads, mask)
        cb_ref[...] = cb
        acc_ref[...] = jnp.concatenate(outs, axis=0)

    cb = cb_ref[...]
    outs = [jnp.zeros((tq, HEAD_DIM), F32)] * heads
    for c in range(npg):
        cb, outs = block(kp_refs[c], vp_refs[c], cb, outs, None)
    cb_ref[...] = cb
    acc_ref[...] += jnp.concatenate(outs, axis=0)

    @pl.when(j == pl.num_programs(1) - 1)
    def _():
        o_ref[...] = acc_ref[...]


def _attn_sample(q_s, bias_rows, k_new, v_new, cache_k, cache_v, page_table, *, heads, tq):
    db, n_pages = page_table.shape
    rows = heads * tq
    npg = PAGES_PER_STEP
    assert n_pages % npg == 0 and tq & (tq - 1) == 0
    n_new = k_new.shape[1] // heads

    def page_map(c):
        return lambda b, j, pt: (pt[b * n_pages + (n_pages - 1 - (j * npg + c))], 0, 0)

    page_specs = [pl.BlockSpec((None, PAGE * heads, HEAD_DIM), page_map(c)) for c in range(npg)]
    uo_r = lax.broadcasted_iota(jnp.int32, (PAGE, 2 * PAGE), 0)
    uo_c = lax.broadcasted_iota(jnp.int32, (PAGE, 2 * PAGE), 1)
    uo = jnp.where((uo_r > uo_c) | (uo_c >= PAGE), 1.0, 0.0).astype(BF16)
    grid_spec = pltpu.PrefetchScalarGridSpec(
        num_scalar_prefetch=1,
        grid=(db, n_pages // npg),
        in_specs=[pl.BlockSpec((None, rows, HEAD_DIM), lambda b, j, pt: (b, 0, 0)),
                  pl.BlockSpec((rows, PAGE), lambda b, j, pt: (0, 0)),
                  pl.BlockSpec((None, n_new * heads, HEAD_DIM), lambda b, j, pt: (b, 0, 0)),
                  pl.BlockSpec((None, n_new * heads, HEAD_DIM), lambda b, j, pt: (b, 0, 0))]
                 + page_specs + page_specs
                 + [pl.BlockSpec((PAGE, 2 * PAGE), lambda b, j, pt: (0, 0))],
        out_specs=pl.BlockSpec((None, rows, HEAD_DIM), lambda b, j, pt: (b, 0, 0)),
        scratch_shapes=[pltpu.VMEM((rows, HEAD_DIM), F32),
                        pltpu.VMEM((rows, PAGE), F32),
                        pltpu.VMEM((PAGE * heads, HEAD_DIM), F32),
                        pltpu.VMEM((PAGE * heads, HEAD_DIM), F32)],
    )
    return pl.pallas_call(
        functools.partial(_attn_sample_kernel, heads=heads, tq=tq, n_new=n_new),
        grid_spec=grid_spec,
        out_shape=jax.ShapeDtypeStruct((db, rows, HEAD_DIM), F32),
        compiler_params=_params(("parallel", "arbitrary")),
    )(page_table.reshape(-1), q_s, bias_rows, k_new, v_new,
      *([cache_k] * npg), *([cache_v] * npg), uo)


def _pool_prompt_kernel(u_ref, halo_ref, w_ref, sc_ref, o_ref, ext_ref, *, front, tr, gdim):
    i = pl.program_id(0)
    halo = halo_ref[...]
    ext_ref[0:POOL_HALO, :] = jnp.where(i > 0, halo, 0.0)
    ext_ref[POOL_HALO:, :] = u_ref[...]
    pos = i * tr + lax.broadcasted_iota(jnp.int32, (tr, gdim), 0) - front
    for g, w in enumerate(POOL_WINDOWS):
        cols = slice(g * gdim, (g + 1) * gdim)
        u = ext_ref[POOL_HALO:, cols]
        win = u
        for k in range(1, w):
            win = win + ext_ref[POOL_HALO - k:POOL_HALO - k + tr, cols]
        cnt = jnp.clip(pos + 1, 1, w).astype(F32)
        pooled = (win / cnt - u).astype(BF16)
        y = jnp.dot(pooled, w_ref[g].astype(BF16), preferred_element_type=F32)
        o_ref[:, cols] = (y * sc_ref[:, cols]).astype(o_ref.dtype)


def _pool_prompt(u, w_pool, pool_scale, *, rows, u_col0, front, tr):
    ng, gdim, _ = w_pool.shape
    width = ng * gdim
    assert u_col0 % width == 0
    cb = u_col0 // width
    hb = tr // POOL_HALO
    return pl.pallas_call(
        functools.partial(_pool_prompt_kernel, front=front, tr=tr, gdim=gdim),
        grid=(rows // tr,),
        in_specs=[pl.BlockSpec((tr, width), lambda i: (i, cb)),
                  pl.BlockSpec((POOL_HALO, width), lambda i: (jnp.maximum(i * hb - 1, 0), cb)),
                  pl.BlockSpec((ng, gdim, gdim), lambda i: (0, 0, 0)),
                  pl.BlockSpec((1, width), lambda i: (0, 0))],
        out_specs=pl.BlockSpec((tr, width), lambda i: (i, 0)),
        out_shape=jax.ShapeDtypeStruct((rows, width), BF16),
        scratch_shapes=[pltpu.VMEM((POOL_HALO + tr, width), F32)],
        compiler_params=_params(("parallel",)),
    )(u, u, w_pool, pool_scale.reshape(1, width))


def _pool_sample_kernel(st_ref, u_ref, w_ref, sc_ref, o_ref, *, gdim, n_new):
    ctx = st_ref.shape[0]
    rows = [st_ref[r] for r in range(ctx)] + [u_ref[r] for r in range(n_new)]
    for g, w in enumerate(POOL_WINDOWS):
        cols = slice(g * gdim, (g + 1) * gdim)
        wg = w_ref[g].astype(BF16)
        for s in range(n_new):
            last = ctx + s
            win = rows[last][:, cols]
            for k in range(1, w):
                win = win + rows[last - k][:, cols]
            pooled = (win / float(w) - rows[last][:, cols]).astype(BF16)
            y = jnp.dot(pooled, wg, preferred_element_type=F32)
            o_ref[s, :, cols] = (y * sc_ref[:, cols]).astype(o_ref.dtype)


def _pool_sample(state_t, u_t, w_pool, pool_scale, *, tb):
    ctx, db, width = state_t.shape
    n_new = u_t.shape[0]
    ng, gdim, _ = w_pool.shape
    assert ctx >= max(POOL_WINDOWS) - 1
    return pl.pallas_call(
        functools.partial(_pool_sample_kernel, gdim=gdim, n_new=n_new),
        grid=(db // tb,),
        in_specs=[pl.BlockSpec((ctx, tb, width), lambda i: (0, i, 0)),
                  pl.BlockSpec((n_new, tb, width), lambda i: (0, i, 0)),
                  pl.BlockSpec((ng, gdim, gdim), lambda i: (0, 0, 0)),
                  pl.BlockSpec((1, width), lambda i: (0, 0))],
        out_specs=pl.BlockSpec((n_new, tb, width), lambda i: (0, i, 0)),
        out_shape=jax.ShapeDtypeStruct((n_new, db, width), BF16),
        compiler_params=_params(("parallel",)),
    )(state_t, u_t, w_pool, pool_scale.reshape(1, width))


def kernel(x_prompt, x_sample, cache_k, cache_v, state_pool, page_table, meta_tokens, g_pre_mix, w_in, gate_bias, sb_bias, w_pool, pool_scale, w_up_attn, w_up_pool, w_out, g_post_mix, g_pre_mlp, w_ff1, w_ff2, g_post_mlp):
    batch, seq, d = x_prompt.shape
    db, ds, _ = x_sample.shape
    depth = w_in.shape[0]
    n_meta = meta_tokens.shape[0]
    n_phys, heads = cache_k.shape[1], cache_k.shape[3]
    sbw = heads * HEAD_DIM
    ng, gdim = w_pool.shape[1], w_pool.shape[2]
    pw = ng * gdim
    assert batch == 1 and depth == 1, "one prompt sequence, one layer"
    assert cache_k.shape[2] == PAGE and cache_k.shape[4] == HEAD_DIM
    assert seq % ABLK == 0 and n_meta <= ABLK and ds <= 8
    assert heads % 8 == 0 and heads % HEAD_GROUP == 0
    assert state_pool.shape[2] == POOL_CTX and sbw == pw

    front = ABLK - n_meta
    rows_p = ABLK + seq
    n_s = db * ds
    rows = rows_p + n_s
    assert n_s % ABLK == 0
    n_p, n_sb = seq // ABLK, n_s // ABLK
    u0, g0 = 2 * sbw, 2 * sbw + pw

    x_head = jnp.concatenate([jnp.zeros((front, d), F32), meta_tokens.astype(F32)], axis=0)
    x_p, x_s = x_prompt[0], x_sample.reshape(n_s, d)

    tm = _pick(rows, 1200, 16)
    tm2 = _pick(rows, 1300, 256)
    tn = 512
    dense = functools.partial(_matmul, tm=tm2, tn=tn)

    h0 = _norm_cast(x_head, x_p, x_s, g_pre_mix[0])
    q16, p, kv16 = _in_proj(h0, w_in[0], gate_bias[0], tm=tm, tn=tn, q_cols=sbw, kv_cols=2 * sbw)

    o_attn_p = _attn_prompt(q16, kv16, sb_bias[0], rows=rows_p, heads=heads)

    tq = 8
    q_s = q16[rows_p:].astype(F32).reshape(db, ds, heads, HEAD_DIM)
    q_s = jnp.pad(q_s.transpose(0, 2, 1, 3), ((0, 0), (0, 0), (0, tq - ds), (0, 0)))
    q_s = q_s.reshape(db, heads * tq, HEAD_DIM)
    k_s = p[rows_p:, :sbw]
    v_s = p[rows_p:, sbw:2 * sbw]
    bias_rows = jnp.broadcast_to(jnp.repeat(sb_bias[0].astype(F32) * LOG2E, tq)[:, None],
                                 (heads * tq, PAGE))
    o_s = _attn_sample(q_s, bias_rows, k_s.reshape(db, ds * heads, HEAD_DIM),
                       v_s.reshape(db, ds * heads, HEAD_DIM),
                       cache_k.reshape(n_phys, PAGE * heads, HEAD_DIM),
                       cache_v.reshape(n_phys, PAGE * heads, HEAD_DIM),
                       page_table, heads=heads, tq=tq)
    o_attn_s = o_s.reshape(db, heads, tq, HEAD_DIM)[:, :, :ds].transpose(0, 2, 1, 3)
    o_attn = jnp.concatenate([o_attn_p, o_attn_s.reshape(n_s, sbw).astype(BF16)], axis=0)

    o_pool_p = _pool_prompt(p, w_pool[0], pool_scale[0], rows=rows_p, u_col0=u0, front=front,
                            tr=_pick(rows_p, 700, 16))
    u_s = p[rows_p:, u0:u0 + pw].reshape(db, ds, pw)
    o_pool_s = _pool_sample(state_pool[0].transpose(1, 0, 2), u_s.transpose(1, 0, 2),
                            w_pool[0], pool_scale[0], tb=_pick(db, 32, 8))
    o_pool = jnp.concatenate([o_pool_p, o_pool_s.transpose(1, 0, 2).reshape(n_s, pw)], axis=0)

    m = _up_gate(o_attn, o_pool, w_up_attn[0], w_up_pool[0], p, g0, tm=tm2, tn=tn // 2)
    mix = dense(m, w_out[0], tk=d, outs=[(F32, None)])
    x1, h2 = _resid_norm2(x_head, x_p, x_s, mix, g_post_mix[0], g_pre_mlp[0])

    dff = w_ff1.shape[2]
    hid = dense(h2, w_ff1[0], tk=d, outs=[(BF16, None)], epilogue="relu2")
    ffo = _matmul(hid, w_ff2[0], tm=tm2, tn=2 * tn, tk=_pick(dff, 2048, 128), outs=[(F32, None)])
    y_p, y_s = _resid_norm_out(x1, ffo, g_post_mlp[0], n_p=n_p, n_sb=n_sb)

    y_prompt = y_p[None]
    y_sample = y_s.reshape(db, ds, d)
    kv_shape = (1, 1, n_meta + seq, heads, HEAD_DIM)
    k_prompt = p[front:rows_p, :sbw].reshape(kv_shape)
    v_prompt = p[front:rows_p, sbw:2 * sbw].reshape(kv_shape)
    pool_prompt = p[rows_p - POOL_CTX:rows_p, u0:u0 + pw][None, None]
    k_sample = k_s.reshape(1, db, ds, heads, HEAD_DIM)
    v_sample = v_s.reshape(1, db, ds, heads, HEAD_DIM)
    pool_sample = jnp.concatenate([state_pool[0, :, ds:], u_s], axis=1)[None]
    return (y_prompt, y_sample, k_prompt, v_prompt, pool_prompt, k_sample, v_sample, pool_sample)
```

```python
import functools
import math

import jax
import jax.numpy as jnp
from jax import lax
from jax.experimental import pallas as pl
from jax.experimental.pallas import tpu as pltpu

HEAD_DIM = 128
PAGE = 128
ABLK = 256
HEAD_GROUP = 4
PAGES_PER_STEP = 8
POOL_WINDOWS = (2, 4, 8, 16)
POOL_CTX = max(POOL_WINDOWS) - 1
POOL_HALO = 16
NORM_EPS = 1e-6
LOG2E = 1.4426950408889634
SB_SCALE2 = LOG2E / math.sqrt(HEAD_DIM)
MASKED = -1e30
VMEM_LIMIT = 56 * 1024 * 1024

F32 = jnp.float32
BF16 = jnp.bfloat16
NT_DIMS = (((1,), (1,)), ((), ()))


def _params(sem, vmem=VMEM_LIMIT):
    return pltpu.CompilerParams(dimension_semantics=sem, vmem_limit_bytes=vmem)


def _pick(n, cap, align):
    best = None
    for d in range(align, min(n, cap) + 1, align):
        if n % d == 0:
            best = d
    assert best is not None, (n, cap, align)
    return best


def _log2_fail(z):
    nz = -z
    e = jnp.exp2(jnp.minimum(z, nz))
    return jnp.minimum(nz, 0.0) - jnp.log(1.0 + e) * LOG2E


def _rms(x, g):
    return x * lax.rsqrt(jnp.mean(x * x, axis=-1, keepdims=True) + NORM_EPS) * g


def _tile_maps(n_p, n_sb):
    prompt = lambda i: (jnp.clip(i - 1, 0, n_p - 1), 0)
    sample = lambda i: (jnp.clip(i - 1 - n_p, 0, n_sb - 1), 0)
    return prompt, sample


def _x_specs(d, n_p, n_sb):
    prompt, sample = _tile_maps(n_p, n_sb)
    return [pl.BlockSpec((ABLK, d), lambda i: (0, 0)),
            pl.BlockSpec((ABLK, d), prompt),
            pl.BlockSpec((ABLK, d), sample)]


def _for_x_tile(n_p, head_ref, xp_ref, xs_ref, fn):
    i = pl.program_id(0)

    @pl.when(i == 0)
    def _():
        fn(head_ref[...])

    @pl.when((i >= 1) & (i <= n_p))
    def _():
        fn(xp_ref[...])

    @pl.when(i > n_p)
    def _():
        fn(xs_ref[...])


def _norm_cast_kernel(head_ref, xp_ref, xs_ref, g_ref, h_ref, *, n_p):
    def fn(x):
        h_ref[...] = _rms(x, g_ref[...]).astype(h_ref.dtype)
    _for_x_tile(n_p, head_ref, xp_ref, xs_ref, fn)


def _norm_cast(x_head, x_prompt, x_sample, g):
    d = x_head.shape[1]
    n_p, n_sb = x_prompt.shape[0] // ABLK, x_sample.shape[0] // ABLK
    n_t = 1 + n_p + n_sb
    return pl.pallas_call(
        functools.partial(_norm_cast_kernel, n_p=n_p),
        grid=(n_t,),
        in_specs=_x_specs(d, n_p, n_sb) + [pl.BlockSpec((1, d), lambda i: (0, 0))],
        out_specs=pl.BlockSpec((ABLK, d), lambda i: (i, 0)),
        out_shape=jax.ShapeDtypeStruct((n_t * ABLK, d), BF16),
        compiler_params=_params(("parallel",)),
    )(x_head, x_prompt, x_sample, g.reshape(1, d))


def _resid_norm2_kernel(head_ref, xp_ref, xs_ref, y_ref, g_ref, g2_ref, o_ref, h_ref, *, n_p):
    def fn(x):
        x1 = x + _rms(y_ref[...], g_ref[...])
        o_ref[...] = x1
        h_ref[...] = _rms(x1, g2_ref[...]).astype(h_ref.dtype)
    _for_x_tile(n_p, head_ref, xp_ref, xs_ref, fn)


def _resid_norm2(x_head, x_prompt, x_sample, y, g, g2):
    r, d = y.shape
    n_p, n_sb = x_prompt.shape[0] // ABLK, x_sample.shape[0] // ABLK
    row = pl.BlockSpec((ABLK, d), lambda i: (i, 0))
    vec = pl.BlockSpec((1, d), lambda i: (0, 0))
    return pl.pallas_call(
        functools.partial(_resid_norm2_kernel, n_p=n_p),
        grid=(r // ABLK,),
        in_specs=_x_specs(d, n_p, n_sb) + [row, vec, vec],
        out_specs=[row, row],
        out_shape=[jax.ShapeDtypeStruct((r, d), F32), jax.ShapeDtypeStruct((r, d), BF16)],
        compiler_params=_params(("parallel",)),
    )(x_head, x_prompt, x_sample, y, g.reshape(1, d), g2.reshape(1, d))


def _resid_norm_out_kernel(x_ref, y_ref, g_ref, yp_ref, ys_ref, *, n_p):
    i = pl.program_id(0)

    @pl.when((i >= 1) & (i <= n_p))
    def _():
        yp_ref[...] = x_ref[...] + _rms(y_ref[...], g_ref[...])

    @pl.when(i > n_p)
    def _():
        ys_ref[...] = x_ref[...] + _rms(y_ref[...], g_ref[...])


def _resid_norm_out(x, y, g, *, n_p, n_sb):
    r, d = x.shape
    prompt, sample = _tile_maps(n_p, n_sb)
    row = pl.BlockSpec((ABLK, d), lambda i: (i, 0))
    return pl.pallas_call(
        functools.partial(_resid_norm_out_kernel, n_p=n_p),
        grid=(r // ABLK,),
        in_specs=[row, row, pl.BlockSpec((1, d), lambda i: (0, 0))],
        out_specs=[pl.BlockSpec((ABLK, d), prompt), pl.BlockSpec((ABLK, d), sample)],
        out_shape=[jax.ShapeDtypeStruct((n_p * ABLK, d), F32),
                   jax.ShapeDtypeStruct((n_sb * ABLK, d), F32)],
        compiler_params=_params(("arbitrary",)),
    )(x, y, g.reshape(1, d))


def _column_halves(width):
    half = width // 2
    return [slice(0, half), slice(half, width)]


def _mm_kernel(a_ref, w_ref, o_ref, *, epilogue, nk):
    if nk > 1:
        @pl.when(pl.program_id(2) == 0)
        def _():
            o_ref[...] = jnp.zeros_like(o_ref)

    for cols in _column_halves(o_ref.shape[1]):
        part = jnp.dot(a_ref[...], w_ref[:, cols].astype(BF16), preferred_element_type=F32)
        if nk > 1:
            o_ref[:, cols] += part
        else:
            if epilogue == "relu2":
                part = jnp.square(jnp.maximum(part, 0.0))
            o_ref[:, cols] = part.astype(o_ref.dtype)


def _matmul(a, w, *, tm, tn, tk, out_dtype, epilogue=None):
    r, kdim = a.shape
    n = w.shape[1]
    nk = kdim // tk
    assert nk == 1 or (out_dtype == F32 and epilogue is None)
    return pl.pallas_call(
        functools.partial(_mm_kernel, epilogue=epilogue, nk=nk),
        grid=(r // tm, n // tn, nk),
        in_specs=[pl.BlockSpec((tm, tk), lambda i, j, k: (i, k)),
                  pl.BlockSpec((tk, tn), lambda i, j, k: (k, j))],
        out_specs=pl.BlockSpec((tm, tn), lambda i, j, k: (i, j)),
        out_shape=jax.ShapeDtypeStruct((r, n), out_dtype),
        compiler_params=_params(("parallel", "parallel", "arbitrary")),
    )(a, w)


def _in_proj_kernel(a_ref, w_ref, b_ref, q16_ref, p_ref, kv16_ref, *, q_tiles, kv_tiles, gate_tile0):
    j = pl.program_id(1)
    halves = _column_halves(p_ref.shape[1])

    def product(cols):
        return jnp.dot(a_ref[...], w_ref[:, cols].astype(BF16), preferred_element_type=F32)

    @pl.when(j < q_tiles)
    def _():
        for cols in halves:
            q16_ref[:, cols] = (product(cols) * SB_SCALE2).astype(BF16)

    @pl.when((j >= q_tiles) & (j < q_tiles + kv_tiles))
    def _():
        for cols in halves:
            part = product(cols)
            p_ref[:, cols] = part
            kv16_ref[:, cols] = part.astype(BF16)

    @pl.when((j >= q_tiles + kv_tiles) & (j < gate_tile0))
    def _():
        for cols in halves:
            p_ref[:, cols] = product(cols)

    @pl.when(j >= gate_tile0)
    def _():
        for cols in halves:
            p_ref[:, cols] = 1.0 / (1.0 + jnp.exp(-(product(cols) + b_ref[:, cols])))


def _in_proj(h, w_in, gate_bias, *, tm, tn, q_cols, kv_cols):
    r, d = h.shape
    n = w_in.shape[1]
    n_gate = gate_bias.shape[0]
    q_tiles, kv_tiles, gate_tile0 = q_cols // tn, kv_cols // tn, (n - n_gate) // tn
    n_tiles = n // tn
    return pl.pallas_call(
        functools.partial(_in_proj_kernel, q_tiles=q_tiles, kv_tiles=kv_tiles,
                          gate_tile0=gate_tile0),
        grid=(r // tm, n_tiles),
        in_specs=[pl.BlockSpec((tm, d), lambda i, j: (i, 0)),
                  pl.BlockSpec((d, tn), lambda i, j: (0, j)),
                  pl.BlockSpec((1, tn), lambda i, j: (0, jnp.maximum(j - gate_tile0, 0)))],
        out_specs=[pl.BlockSpec((tm, tn), lambda i, j: (i, jnp.minimum(j, q_tiles - 1))),
                   pl.BlockSpec((tm, tn), lambda i, j: (i, jnp.maximum(j - q_tiles, 0))),
                   pl.BlockSpec((tm, tn),
                                lambda i, j: (i, jnp.clip(j - q_tiles, 0, kv_tiles - 1)))],
        out_shape=[jax.ShapeDtypeStruct((r, q_cols), BF16),
                   jax.ShapeDtypeStruct((r, n - q_cols), F32),
                   jax.ShapeDtypeStruct((r, kv_cols), BF16)],
        compiler_params=_params(("parallel", "arbitrary")),
    )(h, w_in, gate_bias.reshape(1, n_gate))


def _up_gate_kernel(oa_ref, op_ref, wa_ref, wp_ref, ga_ref, gp_ref, o_ref):
    ya = jnp.dot(oa_ref[...], wa_ref[...].astype(BF16), preferred_element_type=F32)
    yp = jnp.dot(op_ref[...], wp_ref[...].astype(BF16), preferred_element_type=F32)
    o_ref[...] = (ga_ref[...] * ya + gp_ref[...] * yp).astype(o_ref.dtype)


def _up_gate(o_attn, o_pool, w_up_attn, w_up_pool, gates, gate_col0, *, tm, tn):
    r, c = o_attn.shape
    d = w_up_attn.shape[1]
    ga0 = gate_col0 // tn
    gp0 = (gate_col0 + d) // tn
    return pl.pallas_call(
        _up_gate_kernel,
        grid=(r // tm, d // tn),
        in_specs=[pl.BlockSpec((tm, c), lambda i, j: (i, 0)),
                  pl.BlockSpec((tm, c), lambda i, j: (i, 0)),
                  pl.BlockSpec((c, tn), lambda i, j: (0, j)),
                  pl.BlockSpec((c, tn), lambda i, j: (0, j)),
                  pl.BlockSpec((tm, tn), lambda i, j: (i, ga0 + j)),
                  pl.BlockSpec((tm, tn), lambda i, j: (i, gp0 + j))],
        out_specs=pl.BlockSpec((tm, tn), lambda i, j: (i, j)),
        out_shape=jax.ShapeDtypeStruct((r, d), BF16),
        compiler_params=_params(("parallel", "parallel")),
    )(o_attn, o_pool, w_up_attn, w_up_pool, gates, gates)


def _attn_prompt_kernel(bias_ref, q_ref, k_ref, v_ref, tri_ref, dmask_ref, o_ref,
                        vt_ref, z_ref, a_ref, acc_ref, *, nblk):
    hg = pl.program_id(0)
    i = pl.program_id(1)
    lanes = [slice(g * HEAD_DIM, (g + 1) * HEAD_DIM) for g in range(HEAD_GROUP)]

    @pl.when(i == 0)
    def _():
        def transpose_block(c, carry):
            st = pl.multiple_of(c * ABLK, ABLK)
            for g in range(HEAD_GROUP):
                vt_ref[g, :, pl.ds(st, ABLK)] = (
                    v_ref[pl.ds(st, ABLK), lanes[g]].astype(F32).T.astype(BF16))
            return carry
        lax.fori_loop(0, nblk, transpose_block, 0)

    heads = range(HEAD_GROUP)
    bias2 = [bias_ref[hg * HEAD_GROUP + g] * LOG2E for g in heads]

    def scores(kb):
        st = pl.multiple_of(kb * ABLK, ABLK)
        return tuple(lax.dot_general(k_ref[pl.ds(st, ABLK), lanes[g]], q_ref[:, lanes[g]],
                                     NT_DIMS, preferred_element_type=F32) + bias2[g]
                     for g in heads)

    def weights(z, cb):
        lf = [_log2_fail(zg) for zg in z]
        later = [jnp.dot(tri_ref[...], lfg.astype(BF16), preferred_element_type=F32)
                 for lfg in lf]
        a = tuple(jnp.exp2(z[g] + lf[g] + (cb[g] + later[g])).astype(BF16) for g in heads)
        cb = tuple(cb[g] + (later[g][0:1, :] + lf[g][0:1, :]) for g in heads)
        return a, cb

    def attend(kb):
        st = pl.multiple_of(kb * ABLK, ABLK)
        for g in heads:
            acc_ref[g] += jnp.dot(vt_ref[g, :, pl.ds(st, ABLK)], a_ref[g],
                                  preferred_element_type=F32)

    def stash(z, a):
        for g in heads:
            z_ref[g] = z[g]
            a_ref[g] = a[g]

    acc_ref[...] = jnp.zeros_like(acc_ref)
    cb = (jnp.zeros((1, ABLK), F32),) * HEAD_GROUP
    a, cb = weights([zg + dmask_ref[...] for zg in scores(i)], cb)
    stash(scores(jnp.maximum(i - 1, 0)), a)

    def body(t, cb):
        kb = i - 1 - t
        attend(kb + 1)
        z_next = scores(jnp.maximum(kb - 1, 0))
        a, cb = weights([z_ref[g] for g in heads], cb)
        stash(z_next, a)
        return cb

    lax.fori_loop(0, i, body, cb)
    attend(0)
    for g in heads:
        o_ref[:, lanes[g]] = acc_ref[g].T.astype(o_ref.dtype)


def _attn_prompt(q16, kv16, sb_bias, *, rows, heads):
    nblk = rows // ABLK
    ngrp = heads // HEAD_GROUP
    gw = HEAD_GROUP * HEAD_DIM
    r = lax.broadcasted_iota(jnp.int32, (ABLK, ABLK), 0)
    c = lax.broadcasted_iota(jnp.int32, (ABLK, ABLK), 1)
    tri = jnp.where(c > r, 1.0, 0.0).astype(BF16)
    dmask = jnp.where(r < c, 0.0, MASKED).astype(F32)
    return pl.pallas_call(
        functools.partial(_attn_prompt_kernel, nblk=nblk),
        grid=(ngrp, nblk),
        in_specs=[pl.BlockSpec(memory_space=pltpu.SMEM),
                  pl.BlockSpec((ABLK, gw), lambda h, i: (i, h)),
                  pl.BlockSpec((rows, gw), lambda h, i: (0, h)),
                  pl.BlockSpec((rows, gw), lambda h, i: (0, ngrp + h)),
                  pl.BlockSpec((ABLK, ABLK), lambda h, i: (0, 0)),
                  pl.BlockSpec((ABLK, ABLK), lambda h, i: (0, 0))],
        out_specs=pl.BlockSpec((ABLK, gw), lambda h, i: (i, h)),
        out_shape=jax.ShapeDtypeStruct((rows, heads * HEAD_DIM), BF16),
        scratch_shapes=[pltpu.VMEM((HEAD_GROUP, HEAD_DIM, rows), BF16),
                        pltpu.VMEM((HEAD_GROUP, ABLK, ABLK), F32),
                        pltpu.VMEM((HEAD_GROUP, ABLK, ABLK), BF16),
                        pltpu.VMEM((HEAD_GROUP, HEAD_DIM, ABLK), F32)],
        compiler_params=_params(("parallel", "arbitrary")),
    )(sb_bias, q16, kv16, kv16, tri, dmask)


def _attn_sample_kernel(pt_ref, q_ref, bias_ref, kn_ref, vn_ref, *rest, heads, tq, n_new):
    npg = PAGES_PER_STEP
    kp_refs, vp_refs = rest[:npg], rest[npg:2 * npg]
    uo_ref, o_ref, acc_ref, cb_ref, kbuf_ref, vbuf_ref = rest[2 * npg:]
    del pt_ref
    j = pl.program_id(1)
    uo = uo_ref[...]
    bias2 = bias_ref[...]

    def by_head(ref):
        x = ref[...].reshape(PAGE, heads, HEAD_DIM)
        return pltpu.einshape("khd->hkd", x)

    def block(k_ref, v_ref, cb, outs, mask):
        kh = by_head(k_ref)
        vh = by_head(v_ref)
        z = jnp.concatenate(
            [lax.dot_general(q_ref[hh * tq:(hh + 1) * tq, :].astype(BF16),
                             kh[hh].astype(BF16),
                             NT_DIMS, preferred_element_type=F32)
             for hh in range(heads)], axis=0) + bias2
        if mask is not None:
            z = z + mask
        lf = _log2_fail(z)
        r = jnp.dot(lf.astype(BF16), uo, preferred_element_type=F32)
        a = jnp.exp2(z + lf + (cb + r[:, :PAGE]))
        outs = [o + jnp.dot(a[hh * tq:(hh + 1) * tq, :].astype(BF16),
                            vh[hh].astype(BF16),
                            preferred_element_type=F32)
                for hh, o in enumerate(outs)]
        return cb + r[:, PAGE:], outs

    @pl.when(j == 0)
    def _():
        kbuf_ref[...] = jnp.zeros_like(kbuf_ref)
        vbuf_ref[...] = jnp.zeros_like(vbuf_ref)
        kbuf_ref[0:n_new * heads, :] = kn_ref[...]
        vbuf_ref[0:n_new * heads, :] = vn_ref[...]
        row = lax.broadcasted_iota(jnp.int32, (heads * tq, PAGE), 0)
        col = lax.broadcasted_iota(jnp.int32, (heads * tq, PAGE), 1)
        mask = jnp.where(col < (row & (tq - 1)), 0.0, MASKED)
        cb, outs = block(kbuf_ref, vbuf_ref, jnp.zeros((heads * tq, PAGE), F32),
                         [jnp.zeros((tq, HEAD_DIM), F32)] * heads, mask)
        cb_ref[...] = cb
        acc_ref[...] = jnp.concatenate(outs, axis=0)

    cb = cb_ref[...]
    outs = [jnp.zeros((tq, HEAD_DIM), F32)] * heads
    for c in range(npg):
        cb, outs = block(kp_refs[c], vp_refs[c], cb, outs, None)
    cb_ref[...] = cb
    acc_ref[...] += jnp.concatenate(outs, axis=0)

    @pl.when(j == pl.num_programs(1) - 1)
    def _():
        o_ref[...] = acc_ref[...]


def _attn_sample(q_s, bias_rows, k_new, v_new, cache_k, cache_v, page_table, *, heads, tq):
    db, n_pages = page_table.shape
    rows = heads * tq
    npg = PAGES_PER_STEP
    assert n_pages % npg == 0 and tq & (tq - 1) == 0
    n_new = k_new.shape[1] // heads

    def page_map(c):
        return lambda b, j, pt: (pt[b * n_pages + (n_pages - 1 - (j * npg + c))], 0, 0)

    page_specs = [pl.BlockSpec((None, PAGE * heads, HEAD_DIM), page_map(c)) for c in range(npg)]
    uo_r = lax.broadcasted_iota(jnp.int32, (PAGE, 2 * PAGE), 0)
    uo_c = lax.broadcasted_iota(jnp.int32, (PAGE, 2 * PAGE), 1)
    uo = jnp.where((uo_r > uo_c) | (uo_c >= PAGE), 1.0, 0.0).astype(BF16)
    grid_spec = pltpu.PrefetchScalarGridSpec(
        num_scalar_prefetch=1,
        grid=(db, n_pages // npg),
        in_specs=[pl.BlockSpec((None, rows, HEAD_DIM), lambda b, j, pt: (b, 0, 0)),
                  pl.BlockSpec((rows, PAGE), lambda b, j, pt: (0, 0)),
                  pl.BlockSpec((None, n_new * heads, HEAD_DIM), lambda b, j, pt: (b, 0, 0)),
                  pl.BlockSpec((None, n_new * heads, HEAD_DIM), lambda b, j, pt: (b, 0, 0))]
                 + page_specs + page_specs
                 + [pl.BlockSpec((PAGE, 2 * PAGE), lambda b, j, pt: (0, 0))],
        out_specs=pl.BlockSpec((None, rows, HEAD_DIM), lambda b, j, pt: (b, 0, 0)),
        scratch_shapes=[pltpu.VMEM((rows, HEAD_DIM), F32),
                        pltpu.VMEM((rows, PAGE), F32),
                        pltpu.VMEM((PAGE * heads, HEAD_DIM), F32),
                        pltpu.VMEM((PAGE * heads, HEAD_DIM), F32)],
    )
    return pl.pallas_call(
        functools.partial(_attn_sample_kernel, heads=heads, tq=tq, n_new=n_new),
        grid_spec=grid_spec,
        out_shape=jax.ShapeDtypeStruct((db, rows, HEAD_DIM), F32),
        compiler_params=_params(("parallel", "arbitrary")),
    )(page_table.reshape(-1), q_s, bias_rows, k_new, v_new,
      *([cache_k] * npg), *([cache_v] * npg), uo)


def _pool_prompt_kernel(u_ref, halo_ref, w_ref, sc_ref, o_ref, ext_ref, *, front, tr, gdim):
    i = pl.program_id(0)
    halo = halo_ref[...]
    ext_ref[0:POOL_HALO, :] = jnp.where(i > 0, halo, 0.0)
    ext_ref[POOL_HALO:, :] = u_ref[...]
    pos = i * tr + lax.broadcasted_iota(jnp.int32, (tr, gdim), 0) - front
    for g, w in enumerate(POOL_WINDOWS):
        cols = slice(g * gdim, (g + 1) * gdim)
        u = ext_ref[POOL_HALO:, cols]
        win = u
        for k in range(1, w):
            win = win + ext_ref[POOL_HALO - k:POOL_HALO - k + tr, cols]
        cnt = jnp.clip(pos + 1, 1, w).astype(F32)
        pooled = (win / cnt - u).astype(BF16)
        y = jnp.dot(pooled, w_ref[g].astype(BF16), preferred_element_type=F32)
        o_ref[:, cols] = (y * sc_ref[:, cols]).astype(o_ref.dtype)


def _pool_prompt(u, w_pool, pool_scale, *, rows, u_col0, front, tr):
    ng, gdim, _ = w_pool.shape
    width = ng * gdim
    assert u_col0 % width == 0
    cb = u_col0 // width
    hb = tr // POOL_HALO
    return pl.pallas_call(
        functools.partial(_pool_prompt_kernel, front=front, tr=tr, gdim=gdim),
        grid=(rows // tr,),
        in_specs=[pl.BlockSpec((tr, width), lambda i: (i, cb)),
                  pl.BlockSpec((POOL_HALO, width), lambda i: (jnp.maximum(i * hb - 1, 0), cb)),
                  pl.BlockSpec((ng, gdim, gdim), lambda i: (0, 0, 0)),
                  pl.BlockSpec((1, width), lambda i: (0, 0))],
        out_specs=pl.BlockSpec((tr, width), lambda i: (i, 0)),
        out_shape=jax.ShapeDtypeStruct((rows, width), BF16),
        scratch_shapes=[pltpu.VMEM((POOL_HALO + tr, width), F32)],
        compiler_params=_params(("parallel",)),
    )(u, u, w_pool, pool_scale.reshape(1, width))


def _pool_sample_kernel(st_ref, u_ref, w_ref, sc_ref, o_ref, *, gdim, n_new):
    ctx = st_ref.shape[0]
    rows = [st_ref[r] for r in range(ctx)] + [u_ref[r] for r in range(n_new)]
    for g, w in enumerate(POOL_WINDOWS):
        cols = slice(g * gdim, (g + 1) * gdim)
        wg = w_ref[g].astype(BF16)
        for s in range(n_new):
            last = ctx + s
            win = rows[last][:, cols]
            for k in range(1, w):
                win = win + rows[last - k][:, cols]
            pooled = (win / float(w) - rows[last][:, cols]).astype(BF16)
            y = jnp.dot(pooled, wg, preferred_element_type=F32)
            o_ref[s, :, cols] = (y * sc_ref[:, cols]).astype(o_ref.dtype)


def _pool_sample(state_t, u_t, w_pool, pool_scale, *, tb):
    ctx, db, width = state_t.shape
    n_new = u_t.shape[0]
    ng, gdim, _ = w_pool.shape
    assert ctx >= max(POOL_WINDOWS) - 1
    return pl.pallas_call(
        functools.partial(_pool_sample_kernel, gdim=gdim, n_new=n_new),
        grid=(db // tb,),
        in_specs=[pl.BlockSpec((ctx, tb, width), lambda i: (0, i, 0)),
                  pl.BlockSpec((n_new, tb, width), lambda i: (0, i, 0)),
                  pl.BlockSpec((ng, gdim, gdim), lambda i: (0, 0, 0)),
                  pl.BlockSpec((1, width), lambda i: (0, 0))],
        out_specs=pl.BlockSpec((n_new, tb, width), lambda i: (0, i, 0)),
        out_shape=jax.ShapeDtypeStruct((n_new, db, width), BF16),
        compiler_params=_params(("parallel",)),
    )(state_t, u_t, w_pool, pool_scale.reshape(1, width))


def kernel(x_prompt, x_sample, cache_k, cache_v, state_pool, page_table, meta_tokens, g_pre_mix, w_in, gate_bias, sb_bias, w_pool, pool_scale, w_up_attn, w_up_pool, w_out, g_post_mix, g_pre_mlp, w_ff1, w_ff2, g_post_mlp):
    batch, seq, d = x_prompt.shape
    db, ds, _ = x_sample.shape
    depth = w_in.shape[0]
    n_meta = meta_tokens.shape[0]
    n_phys, heads = cache_k.shape[1], cache_k.shape[3]
    sbw = heads * HEAD_DIM
    ng, gdim = w_pool.shape[1], w_pool.shape[2]
    pw = ng * gdim
    assert batch == 1 and depth == 1, "one prompt sequence, one layer"
    assert cache_k.shape[2] == PAGE and cache_k.shape[4] == HEAD_DIM
    assert seq % ABLK == 0 and n_meta <= ABLK and ds <= 8
    assert heads % 8 == 0 and heads % HEAD_GROUP == 0
    assert state_pool.shape[2] == POOL_CTX and sbw == pw

    front = ABLK - n_meta
    rows_p = ABLK + seq
    n_s = db * ds
    rows = rows_p + n_s
    assert n_s % ABLK == 0
    n_p, n_sb = seq // ABLK, n_s // ABLK
    u0, g0 = 2 * sbw, 2 * sbw + pw

    x_head = jnp.concatenate([jnp.zeros((front, d), F32), meta_tokens.astype(F32)], axis=0)
    x_p, x_s = x_prompt[0], x_sample.reshape(n_s, d)

    tm = _pick(rows, 1300, ABLK)
    tn = 512
    dense = functools.partial(_matmul, tm=tm, tn=tn)

    h0 = _norm_cast(x_head, x_p, x_s, g_pre_mix[0])
    q16, p, kv16 = _in_proj(h0, w_in[0], gate_bias[0], tm=tm, tn=tn, q_cols=sbw, kv_cols=2 * sbw)

    o_attn_p = _attn_prompt(q16, kv16, sb_bias[0], rows=rows_p, heads=heads)

    tq = 8
    q_s = q16[rows_p:].astype(F32).reshape(db, ds, heads, HEAD_DIM)
    q_s = jnp.pad(q_s.transpose(0, 2, 1, 3), ((0, 0), (0, 0), (0, tq - ds), (0, 0)))
    q_s = q_s.reshape(db, heads * tq, HEAD_DIM)
    k_s = p[rows_p:, :sbw]
    v_s = p[rows_p:, sbw:2 * sbw]
    bias_rows = jnp.broadcast_to(jnp.repeat(sb_bias[0].astype(F32) * LOG2E, tq)[:, None],
                                 (heads * tq, PAGE))
    o_s = _attn_sample(q_s, bias_rows, k_s.reshape(db, ds * heads, HEAD_DIM),
                       v_s.reshape(db, ds * heads, HEAD_DIM),
                       cache_k.reshape(n_phys, PAGE * heads, HEAD_DIM),
                       cache_v.reshape(n_phys, PAGE * heads, HEAD_DIM),
                       page_table, heads=heads, tq=tq)
    o_attn_s = o_s.reshape(db, heads, tq, HEAD_DIM)[:, :, :ds].transpose(0, 2, 1, 3)
    o_attn = jnp.concatenate([o_attn_p, o_attn_s.reshape(n_s, sbw).astype(BF16)], axis=0)

    o_pool_p = _pool_prompt(p, w_pool[0], pool_scale[0], rows=rows_p, u_col0=u0, front=front,
                            tr=_pick(rows_p, 700, 16))
    u_s = p[rows_p:, u0:u0 + pw].reshape(db, ds, pw)
    o_pool_s = _pool_sample(state_pool[0].transpose(1, 0, 2), u_s.transpose(1, 0, 2),
                            w_pool[0], pool_scale[0], tb=_pick(db, 32, 8))
    o_pool = jnp.concatenate([o_pool_p, o_pool_s.transpose(1, 0, 2).reshape(n_s, pw)], axis=0)

    m = _up_gate(o_attn, o_pool, w_up_attn[0], w_up_pool[0], p, g0, tm=tm, tn=tn // 2)
    mix = dense(m, w_out[0], tk=d, out_dtype=F32)
    x1, h2 = _resid_norm2(x_head, x_p, x_s, mix, g_post_mix[0], g_pre_mlp[0])

    dff = w_ff1.shape[2]
    hid = dense(h2, w_ff1[0], tk=d, out_dtype=BF16, epilogue="relu2")
    ffo = _matmul(hid, w_ff2[0], tm=tm, tn=2 * tn, tk=_pick(dff, 2048, 128), out_dtype=F32)
    y_p, y_s = _resid_norm_out(x1, ffo, g_post_mlp[0], n_p=n_p, n_sb=n_sb)

    y_prompt = y_p[None]
    y_sample = y_s.reshape(db, ds, d)
    kv_shape = (1, 1, n_meta + seq, heads, HEAD_DIM)
    k_prompt = p[front:rows_p, :sbw].reshape(kv_shape)
    v_prompt = p[front:rows_p, sbw:2 * sbw].reshape(kv_shape)
    pool_prompt = p[rows_p - POOL_CTX:rows_p, u0:u0 + pw][None, None]
    k_sample = k_s.reshape(1, db, ds, heads, HEAD_DIM)
    v_sample = v_s.reshape(1, db, ds, heads, HEAD_DIM)
    pool_sample = jnp.concatenate([state_pool[0, :, ds:], u_s], axis=1)[None]
    return (y_prompt, y_sample, k_prompt, v_prompt, pool_prompt, k_sample, v_sample, pool_sample)
```
